```python
import jax, jax.numpy as jnp
from jax import lax
import numpy as np

D_MODEL = 1024
BATCH = 4
SEQ = 4096
DEPTH = 2

POOL_WIDTH = 512
POOL_WINDOWS = (2, 4, 8, 16)
N_POOL_GROUPS = len(POOL_WINDOWS)
POOL_GROUP = POOL_WIDTH // N_POOL_GROUPS
N_Q_HEADS = 8
N_KV_HEADS = 2
HEAD_DIM = 64
Q_GROUP = N_Q_HEADS // N_KV_HEADS
Q_WIDTH = N_Q_HEADS * HEAD_DIM
KV_WIDTH = N_KV_HEADS * HEAD_DIM
WINDOW = 128
BLOCK = 128
N_BRANCHES = 2
IN_WIDTH = POOL_WIDTH + Q_WIDTH + 2 * KV_WIDTH + N_BRANCHES * D_MODEL
D_FF = ((8 * D_MODEL + 3 * 256 - 1) // (3 * 256)) * 256
EPS = 1e-6

kernel_name = "hybrid_pool_swa_gated_encoder"


def rms_norm(x, gain):
    xf = x.astype(jnp.float32)
    y = xf * lax.rsqrt(jnp.mean(xf * xf, axis=-1, keepdims=True) + EPS)
    return (y * gain.astype(jnp.float32)).astype(x.dtype)


def alibi_slopes():
    h = jnp.arange(1, N_Q_HEADS + 1, dtype=jnp.float32)
    return jnp.exp2(-8.0 * h / N_Q_HEADS)


def multiscale_pool(z, w_group, scale):
    B, S, _ = z.shape
    zf = z.astype(jnp.float32)
    cs = jnp.pad(jnp.cumsum(zf, axis=1), ((0, 0), (1, 0), (0, 0)))
    t = jnp.arange(S)
    outs = []
    for g, w in enumerate(POOL_WINDOWS):
        sl = slice(g * POOL_GROUP, (g + 1) * POOL_GROUP)
        lo = jnp.clip(t - w // 2, 0, S)
        hi = jnp.clip(t + w // 2, 0, S)
        csg = cs[..., sl]
        count = (hi - lo).astype(jnp.float32)[None, :, None]
        mean = (csg[:, hi] - csg[:, lo]) / count
        outs.append(mean - zf[..., sl])
    d = jnp.stack(outs, axis=2).astype(z.dtype)
    y = jnp.einsum("bsgc,gcd->bsgd", d, w_group).reshape(B, S, POOL_WIDTH)
    return y * scale


def band_blocks(t, nb):
    B = t.shape[0]
    t = jnp.pad(t, ((0, 0), (BLOCK, BLOCK), (0, 0), (0, 0)))
    t = t.reshape(B, nb + 2, BLOCK, N_KV_HEADS, HEAD_DIM)
    return jnp.concatenate([t[:, :-2], t[:, 1:-1], t[:, 2:]], axis=2)


def windowed_gqa(q, k, v, sink):
    B, S, _ = q.shape
    nb = S // BLOCK
    qb = (q * (HEAD_DIM ** -0.5)).reshape(B, nb, BLOCK, N_KV_HEADS, Q_GROUP, HEAD_DIM)
    kb = band_blocks(k.reshape(B, S, N_KV_HEADS, HEAD_DIM), nb)
    vb = band_blocks(v.reshape(B, S, N_KV_HEADS, HEAD_DIM), nb)
    scores = jnp.einsum("bnqhgd,bnkhd->bnhgqk", qb, kb).astype(jnp.float32)
    a = jnp.arange(BLOCK)[:, None]
    c = jnp.arange(3 * BLOCK)[None, :]
    absdist = jnp.abs(a - c + BLOCK)
    kpos = (jnp.arange(nb)[:, None] - 1) * BLOCK + jnp.arange(3 * BLOCK)[None, :]
    kvalid = (kpos >= 0) & (kpos < S)
    mask = (absdist <= WINDOW)[None, :, :] & kvalid[:, None, :]
    slopes = alibi_slopes().reshape(N_KV_HEADS, Q_GROUP)[:, :, None, None]
    scores = scores - slopes * absdist.astype(jnp.float32)
    scores = jnp.where(mask[None, :, None, None, :, :], scores, -jnp.inf)
    sink_l = sink.astype(jnp.float32).reshape(N_KV_HEADS, Q_GROUP)[:, :, None, None]
    m = jnp.maximum(jnp.max(scores, axis=-1, keepdims=True), sink_l)
    p = jnp.exp(scores - m)
    denom = jnp.sum(p, axis=-1, keepdims=True) + jnp.exp(sink_l - m)
    p = (p / denom).astype(v.dtype)
    out = jnp.einsum("bnhgqk,bnkhd->bnqhgd", p, vb)
    return out.reshape(B, S, Q_WIDTH)


def setup_inputs(seed: int = 0) -> dict:
    key = jax.random.key(seed)
    ks = jax.random.split(key, 16)
    f32 = jnp.float32

    def dense(k, shape, fan_in):
        return jax.random.normal(k, shape, f32) * (fan_in ** -0.5)

    def gain(k, shape):
        return 1.0 + 0.05 * jax.random.normal(k, shape, f32)

    return {
        "x": jax.random.normal(ks[0], (BATCH, SEQ, D_MODEL), f32),
        "norm_mix": gain(ks[1], (DEPTH, D_MODEL)),
        "w_in": dense(ks[2], (DEPTH, D_MODEL, IN_WIDTH), D_MODEL),
        "w_pool_group": dense(ks[3], (DEPTH, N_POOL_GROUPS, POOL_GROUP, POOL_GROUP), POOL_GROUP),
        "pool_scale": gain(ks[4], (DEPTH, POOL_WIDTH)),
        "sink": 0.5 * jax.random.normal(ks[5], (DEPTH, N_Q_HEADS), f32),
        "w_pool_branch": dense(ks[6], (DEPTH, POOL_WIDTH, D_MODEL), POOL_WIDTH),
        "w_attn_branch": dense(ks[7], (DEPTH, Q_WIDTH, D_MODEL), Q_WIDTH),
        "w_out": dense(ks[8], (DEPTH, D_MODEL, D_MODEL), D_MODEL),
        "norm_ffn": gain(ks[9], (DEPTH, D_MODEL)),
        "w_ffn_gate": dense(ks[10], (DEPTH, D_MODEL, D_FF), D_MODEL),
        "w_ffn_up": dense(ks[11], (DEPTH, D_MODEL, D_FF), D_MODEL),
        "w_ffn_down": dense(ks[12], (DEPTH, D_FF, D_MODEL), D_FF),
        "norm_final": gain(ks[13], (D_MODEL,)),
    }


def reference(x, norm_mix, w_in, w_pool_group, pool_scale, sink, w_pool_branch,
              w_attn_branch, w_out, norm_ffn, w_ffn_gate, w_ffn_up, w_ffn_down, norm_final):
    splits = [POOL_WIDTH,
              POOL_WIDTH + Q_WIDTH,
              POOL_WIDTH + Q_WIDTH + KV_WIDTH,
              POOL_WIDTH + Q_WIDTH + 2 * KV_WIDTH,
              POOL_WIDTH + Q_WIDTH + 2 * KV_WIDTH + D_MODEL]
    h = x
    for l in range(DEPTH):
        u = rms_norm(h, norm_mix[l])
        proj = u @ w_in[l]
        z_pool, q, k, v, g_pool, g_attn = jnp.split(proj, splits, axis=-1)
        y_pool = multiscale_pool(z_pool, w_pool_group[l], pool_scale[l]) @ w_pool_branch[l]
        y_attn = windowed_gqa(q, k, v, sink[l]) @ w_attn_branch[l]
        merged = jax.nn.sigmoid(g_pool) * y_pool + jax.nn.sigmoid(g_attn) * y_attn
        h = h + merged @ w_out[l]
        u = rms_norm(h, norm_ffn[l])
        h = h + (jax.nn.silu(u @ w_ffn_gate[l]) * (u @ w_ffn_up[l])) @ w_ffn_down[l]
    return rms_norm(h, norm_final)
```

```python
import functools

import jax
import jax.numpy as jnp
from jax import lax
from jax.experimental import pallas as pl
from jax.experimental.pallas import tpu as pltpu

D_MODEL = 1024
POOL_WIDTH = 512
POOL_WINDOWS = (2, 4, 8, 16)
POOL_GROUP = POOL_WIDTH // len(POOL_WINDOWS)
N_Q_HEADS = 8
N_KV_HEADS = 2
HEAD_DIM = 64
Q_GROUP = N_Q_HEADS // N_KV_HEADS
Q_WIDTH = N_Q_HEADS * HEAD_DIM
KV_WIDTH = N_KV_HEADS * HEAD_DIM
WINDOW = 128
BLOCK = 128
GATE_WIDTH = 2 * D_MODEL
IN_WIDTH = POOL_WIDTH + Q_WIDTH + 2 * KV_WIDTH + GATE_WIDTH
EPS = 1e-6

LANES = 128
POOL_HALO = 8
N_PAIRS = Q_WIDTH // LANES
VMEM_LIMIT_BYTES = 56 * 1024 * 1024

F32 = jnp.float32
BF16 = jnp.bfloat16


def _rms_norm(x, gain):
    ms = jnp.mean(x * x, axis=-1, keepdims=True)
    return x * lax.rsqrt(ms + EPS) * gain


def _dot(a, b):
    return jnp.dot(a, b, preferred_element_type=F32)


def _proj_kernel(x_ref, gain_ref, w_ref, zp_ref, q_ref, kv_ref, g_ref):
    u = _rms_norm(x_ref[...], gain_ref[...]).astype(BF16)
    o = 0
    zp_ref[...] = _dot(u, w_ref[:, o:o + POOL_WIDTH])
    o += POOL_WIDTH
    q_ref[...] = (_dot(u, w_ref[:, o:o + Q_WIDTH]) * (HEAD_DIM ** -0.5)).astype(BF16)
    o += Q_WIDTH
    kv_ref[...] = _dot(u, w_ref[:, o:o + 2 * KV_WIDTH]).astype(BF16)
    o += 2 * KV_WIDTH
    g_ref[...] = _dot(u, w_ref[:, o:o + GATE_WIDTH])


def _proj_call(h, gain, w_in, tm):
    n = h.shape[0]
    row = lambda i: (i, 0)
    const = lambda i: (0, 0)
    return pl.pallas_call(
        _proj_kernel,
        grid=(n // tm,),
        in_specs=[
            pl.BlockSpec((tm, D_MODEL), row),
            pl.BlockSpec((1, D_MODEL), const),
            pl.BlockSpec((D_MODEL, IN_WIDTH), const),
        ],
        out_specs=[
            pl.BlockSpec((tm, POOL_WIDTH), row),
            pl.BlockSpec((tm, Q_WIDTH), row),
            pl.BlockSpec((tm, 2 * KV_WIDTH), row),
            pl.BlockSpec((tm, GATE_WIDTH), row),
        ],
        out_shape=[
            jax.ShapeDtypeStruct((n, POOL_WIDTH), F32),
            jax.ShapeDtypeStruct((n, Q_WIDTH), BF16),
            jax.ShapeDtypeStruct((n, 2 * KV_WIDTH), BF16),
            jax.ShapeDtypeStruct((n, GATE_WIDTH), F32),
        ],
        compiler_params=pltpu.CompilerParams(
            dimension_semantics=("arbitrary",), vmem_limit_bytes=VMEM_LIMIT_BYTES),
        name="proj",
    )(h, gain, w_in)


def _pool_branch(pos, seq, tm, zp_ref, zpp_ref, zpn_ref, wg_ref, ps_ref, zext_ref):
    blocks_per_seq = seq // tm
    zext_ref[0:POOL_HALO, :] = jnp.where(pos > 0, zpp_ref[...], 0.0)
    zext_ref[POOL_HALO:POOL_HALO + tm, :] = zp_ref[...]
    zext_ref[POOL_HALO + tm:, :] = jnp.where(pos < blocks_per_seq - 1, zpn_ref[...], 0.0)
    t = pos * tm + lax.broadcasted_iota(jnp.int32, (tm, 1), 0)
    ys = []
    for g, w in enumerate(POOL_WINDOWS):
        cols = slice(g * POOL_GROUP, (g + 1) * POOL_GROUP)
        acc = None
        for j in range(-(w // 2), w // 2):
            term = zext_ref[POOL_HALO + j:POOL_HALO + j + tm, cols]
            acc = term if acc is None else acc + term
        count = (jnp.minimum(t + w // 2, seq) - jnp.maximum(t - w // 2, 0)).astype(F32)
        d = acc / count - zp_ref[:, cols]
        ys.append(_dot(d.astype(BF16), wg_ref[g]))
    return jnp.concatenate(ys, axis=1) * ps_ref[...]


def _attention(pos, seq, tm, sink_ref, q_ref, kv_ref, kvp_ref, kvn_ref, attn_ref):
    nblk = tm // BLOCK
    kvext = jnp.concatenate([kvp_ref[...], kv_ref[...], kvn_ref[...]], axis=0)
    lane = lax.broadcasted_iota(jnp.int32, (1, LANES), 1)
    lo_half = lane < HEAD_DIM
    a = lax.broadcasted_iota(jnp.int32, (BLOCK, 3 * BLOCK), 0)
    c = lax.broadcasted_iota(jnp.int32, (BLOCK, 3 * BLOCK), 1)
    absdist = jnp.abs(a - c + BLOCK)
    in_band = absdist <= WINDOW
    absdist_f = absdist.astype(F32)
    bias = jnp.concatenate(
        [jnp.where(in_band, -(2.0 ** -(h + 1)) * absdist_f, -jnp.inf) for h in range(N_Q_HEADS)], axis=0)
    sink = jnp.concatenate(
        [jnp.full((BLOCK, 1), sink_ref[h], F32) for h in range(N_Q_HEADS)], axis=0)
    kcol = lax.broadcasted_iota(jnp.int32, (1, 3 * BLOCK), 1)
    for n in range(nblk):
        qb = q_ref[n * BLOCK:(n + 1) * BLOCK, :]
        pairs = [qb[:, j * LANES:(j + 1) * LANES] for j in range(N_PAIRS)]
        zero = jnp.zeros((), BF16)
        qs = jnp.concatenate(
            [jnp.where(lo_half, p, zero) for p in pairs] + [jnp.where(lo_half, zero, p) for p in pairs], axis=0)
        kb = kvext[n * BLOCK:(n + 3) * BLOCK, 0:KV_WIDTH]
        vb = kvext[n * BLOCK:(n + 3) * BLOCK, KV_WIDTH:2 * KV_WIDTH]
        s = lax.dot_general(qs, kb, (((1,), (1,)), ((), ())), preferred_element_type=F32)
        kpos = pos * tm + (n - 1) * BLOCK + kcol
        kvalid = (kpos >= 0) & (kpos < seq)
        s = jnp.where(kvalid, s + bias, -jnp.inf)
        m = jnp.maximum(jnp.max(s, axis=-1, keepdims=True), sink)
        p = jnp.exp(s - m)
        denom = jnp.sum(p, axis=-1, keepdims=True) + jnp.exp(sink - m)
        p = (p * (1.0 / denom)).astype(BF16)
        o = _dot(p, vb)
        half = Q_GROUP * BLOCK
        for j in range(N_PAIRS):
            o_lo = o[j * BLOCK:(j + 1) * BLOCK, :]
            o_hi = o[half + j * BLOCK:half + (j + 1) * BLOCK, :]
            attn_ref[n * BLOCK:(n + 1) * BLOCK, j * LANES:(j + 1) * LANES] = (
                jnp.where(lo_half, o_lo, o_hi).astype(BF16))


def _mix_kernel(seq, tm, sink_ref, h_ref, zp_ref, zpp_ref, zpn_ref, q_ref, kv_ref, kvp_ref, kvn_ref, g_ref,
                wg_ref, ps_ref, wpb_ref, wab_ref, wo_ref, out_ref, zext_ref, attn_ref):
    pos = pl.program_id(0) % (seq // tm)
    y = _pool_branch(pos, seq, tm, zp_ref, zpp_ref, zpn_ref, wg_ref, ps_ref, zext_ref)
    y_pool = _dot(y.astype(BF16), wpb_ref[...])
    _attention(pos, seq, tm, sink_ref, q_ref, kv_ref, kvp_ref, kvn_ref, attn_ref)
    y_attn = _dot(attn_ref[...], wab_ref[...])
    merged = (jax.nn.sigmoid(g_ref[:, 0:D_MODEL]) * y_pool
              + jax.nn.sigmoid(g_ref[:, D_MODEL:GATE_WIDTH]) * y_attn)
    out_ref[...] = h_ref[...] + _dot(merged.astype(BF16), wo_ref[...])


def _mix_call(h, zp, q, kv, gates, sink, wg, ps, wpb, wab, wo, seq, tm):
    n = h.shape[0]
    row = lambda i: (i, 0)
    const2 = lambda i: (0, 0)
    const3 = lambda i: (0, 0, 0)
    zp_per = tm // POOL_HALO
    kv_per = tm // BLOCK
    prev_zp = lambda i: (jnp.maximum(i * zp_per - 1, 0), 0)
    next_zp = lambda i: (jnp.minimum((i + 1) * zp_per, n // POOL_HALO - 1), 0)
    prev_kv = lambda i: (jnp.maximum(i * kv_per - 1, 0), 0)
    next_kv = lambda i: (jnp.minimum((i + 1) * kv_per, n // BLOCK - 1), 0)
    return pl.pallas_call(
        functools.partial(_mix_kernel, seq, tm),
        grid=(n // tm,),
        in_specs=[
            pl.BlockSpec(memory_space=pltpu.SMEM),
            pl.BlockSpec((tm, D_MODEL), row),
            pl.BlockSpec((tm, POOL_WIDTH), row),
            pl.BlockSpec((POOL_HALO, POOL_WIDTH), prev_zp),
            pl.BlockSpec((POOL_HALO, POOL_WIDTH), next_zp),
            pl.BlockSpec((tm, Q_WIDTH), row),
            pl.BlockSpec((tm, 2 * KV_WIDTH), row),
            pl.BlockSpec((BLOCK, 2 * KV_WIDTH), prev_kv),
            pl.BlockSpec((BLOCK, 2 * KV_WIDTH), next_kv),
            pl.BlockSpec((tm, GATE_WIDTH), row),
            pl.BlockSpec((len(POOL_WINDOWS), POOL_GROUP, POOL_GROUP), const3),
            pl.BlockSpec((1, POOL_WIDTH), const2),
            pl.BlockSpec((POOL_WIDTH, D_MODEL), const2),
            pl.BlockSpec((Q_WIDTH, D_MODEL), const2),
            pl.BlockSpec((D_MODEL, D_MODEL), const2),
        ],
        out_specs=pl.BlockSpec((tm, D_MODEL), row),
        out_shape=jax.ShapeDtypeStruct((n, D_MODEL), F32),
        scratch_shapes=[
            pltpu.VMEM((tm + 2 * POOL_HALO, POOL_WIDTH), F32),
            pltpu.VMEM((tm, Q_WIDTH), BF16),
        ],
        compiler_params=pltpu.CompilerParams(
            dimension_semantics=("arbitrary",), vmem_limit_bytes=VMEM_LIMIT_BYTES),
        name="mix",
    )(sink, h, zp, zp, zp, q, kv, kv, kv, gates, wg, ps, wpb, wab, wo)


def _ffn_kernel(final, h_ref, gain_ref, wg_ref, wu_ref, wd_ref, fgain_ref, out_ref):
    h = h_ref[...]
    u = _rms_norm(h, gain_ref[...]).astype(BF16)
    gate = _dot(u, wg_ref[...])
    up = _dot(u, wu_ref[...])
    act = (jax.nn.silu(gate) * up).astype(BF16)
    h = h + _dot(act, wd_ref[...])
    out_ref[...] = _rms_norm(h, fgain_ref[...]) if final else h


def _ffn_call(h, gain, wg, wu, wd, fgain, final, tm):
    n = h.shape[0]
    d_ff = wg.shape[1]
    row = lambda i: (i, 0)
    const = lambda i: (0, 0)
    resident = dict(pipeline_mode=pl.Buffered(1))
    return pl.pallas_call(
        functools.partial(_ffn_kernel, final),
        grid=(n // tm,),
        in_specs=[
            pl.BlockSpec((tm, D_MODEL), row),
            pl.BlockSpec((1, D_MODEL), const),
            pl.BlockSpec((D_MODEL, d_ff), const, **resident),
            pl.BlockSpec((D_MODEL, d_ff), const, **resident),
            pl.BlockSpec((d_ff, D_MODEL), const, **resident),
            pl.BlockSpec((1, D_MODEL), const),
        ],
        out_specs=pl.BlockSpec((tm, D_MODEL), row),
        out_shape=jax.ShapeDtypeStruct((n, D_MODEL), F32),
        compiler_params=pltpu.CompilerParams(
            dimension_semantics=("arbitrary",), vmem_limit_bytes=VMEM_LIMIT_BYTES),
        name="ffn",
    )(h, gain, wg, wu, wd, fgain)


def _pair_heads(w, axis):
    shape = w.shape
    w = w.reshape(shape[:axis] + (N_KV_HEADS, Q_GROUP, HEAD_DIM) + shape[axis + 1:])
    return jnp.swapaxes(w, axis, axis + 1).reshape(shape)


def kernel(x, norm_mix, w_in, w_pool_group, pool_scale, sink, w_pool_branch, w_attn_branch, w_out,
           norm_ffn, w_ffn_gate, w_ffn_up, w_ffn_down, norm_final):
    batch, seq, _ = x.shape
    depth = w_in.shape[0]
    tm = 512
    assert seq % tm == 0 and tm % BLOCK == 0

    q_lo, q_hi = POOL_WIDTH, POOL_WIDTH + Q_WIDTH
    w_in_b = jnp.concatenate(
        [w_in[:, :, :q_lo], _pair_heads(w_in[:, :, q_lo:q_hi], 2), w_in[:, :, q_hi:]], axis=2).astype(BF16)
    wab_b = _pair_heads(w_attn_branch, 1).astype(BF16)

    h = x.reshape(batch * seq, D_MODEL)
    for l in range(depth):
        zp, q, kv, gates = _proj_call(h, norm_mix[l][None, :], w_in_b[l], tm)
        h = _mix_call(h, zp, q, kv, gates, sink[l], w_pool_group[l].astype(BF16), pool_scale[l][None, :],
                      w_pool_branch[l].astype(BF16), wab_b[l], w_out[l].astype(BF16), seq, tm)
        h = _ffn_call(h, norm_ffn[l][None, :], w_ffn_gate[l].astype(BF16), w_ffn_up[l].astype(BF16),
                      w_ffn_down[l].astype(BF16), norm_final[None, :], l == depth - 1, tm)
    return h.reshape(batch, seq, D_MODEL)
```

```python
import functools

import jax
import jax.numpy as jnp
from jax import lax
from jax.experimental import pallas as pl
from jax.experimental.pallas import tpu as pltpu

D_MODEL = 1024
POOL_WIDTH = 512
POOL_WINDOWS = (2, 4, 8, 16)
POOL_GROUP = POOL_WIDTH // len(POOL_WINDOWS)
N_Q_HEADS = 8
N_KV_HEADS = 2
HEAD_DIM = 64
Q_GROUP = N_Q_HEADS // N_KV_HEADS
Q_WIDTH = N_Q_HEADS * HEAD_DIM
KV_WIDTH = N_KV_HEADS * HEAD_DIM
WINDOW = 128
BLOCK = 128
GATE_WIDTH = 2 * D_MODEL
IN_WIDTH = POOL_WIDTH + Q_WIDTH + 2 * KV_WIDTH + GATE_WIDTH
EPS = 1e-6
LOG2E = 1.4426950408889634

LANES = 128
POOL_HALO = 8
N_PAIRS = Q_WIDTH // LANES
VMEM_LIMIT_BYTES = 56 * 1024 * 1024

F32 = jnp.float32
BF16 = jnp.bfloat16


def _rms_norm(x, gain):
    ms = jnp.mean(x * x, axis=-1, keepdims=True)
    return x * lax.rsqrt(ms + EPS) * gain


def _dot(a, b):
    return jnp.dot(a, b, preferred_element_type=F32)


def _proj_kernel(x_ref, gain_ref, w_ref, zp_ref, q_ref, kv_ref, g_ref):
    u = _rms_norm(x_ref[...], gain_ref[...]).astype(BF16)
    o = 0
    zp_ref[...] = _dot(u, w_ref[:, o:o + POOL_WIDTH])
    o += POOL_WIDTH
    q_ref[...] = (_dot(u, w_ref[:, o:o + Q_WIDTH]) * (LOG2E * HEAD_DIM ** -0.5)).astype(BF16)
    o += Q_WIDTH
    kv_ref[...] = _dot(u, w_ref[:, o:o + 2 * KV_WIDTH]).astype(BF16)
    o += 2 * KV_WIDTH
    g_ref[...] = _dot(u, w_ref[:, o:o + GATE_WIDTH])


def _proj_call(h, gain, w_in, tm):
    n = h.shape[0]
    row = lambda i: (i, 0)
    const = lambda i: (0, 0)
    return pl.pallas_call(
        _proj_kernel,
        grid=(n // tm,),
        in_specs=[
            pl.BlockSpec((tm, D_MODEL), row),
            pl.BlockSpec((1, D_MODEL), const),
            pl.BlockSpec((D_MODEL, IN_WIDTH), const),
        ],
        out_specs=[
            pl.BlockSpec((tm, POOL_WIDTH), row),
            pl.BlockSpec((tm, Q_WIDTH), row),
            pl.BlockSpec((tm, 2 * KV_WIDTH), row),
            pl.BlockSpec((tm, GATE_WIDTH), row),
        ],
        out_shape=[
            jax.ShapeDtypeStruct((n, POOL_WIDTH), F32),
            jax.ShapeDtypeStruct((n, Q_WIDTH), BF16),
            jax.ShapeDtypeStruct((n, 2 * KV_WIDTH), BF16),
            jax.ShapeDtypeStruct((n, GATE_WIDTH), F32),
        ],
        compiler_params=pltpu.CompilerParams(
            dimension_semantics=("arbitrary",), vmem_limit_bytes=VMEM_LIMIT_BYTES),
        name="proj",
    )(h, gain, w_in)


def _pool_branch(pos, seq, tm, zp_ref, zpp_ref, zpn_ref, wg_ref, ps_ref, zext_ref):
    blocks_per_seq = seq // tm
    zext_ref[0:POOL_HALO, :] = jnp.where(pos > 0, zpp_ref[...], 0.0)
    zext_ref[POOL_HALO:POOL_HALO + tm, :] = zp_ref[...]
    zext_ref[POOL_HALO + tm:, :] = jnp.where(pos < blocks_per_seq - 1, zpn_ref[...], 0.0)
    t = pos * tm + lax.broadcasted_iota(jnp.int32, (tm, 1), 0)
    ys = []
    for g, w in enumerate(POOL_WINDOWS):
        cols = slice(g * POOL_GROUP, (g + 1) * POOL_GROUP)
        acc = None
        for j in range(-(w // 2), w // 2):
            term = zext_ref[POOL_HALO + j:POOL_HALO + j + tm, cols]
            acc = term if acc is None else acc + term
        count = (jnp.minimum(t + w // 2, seq) - jnp.maximum(t - w // 2, 0)).astype(F32)
        d = acc / count - zp_ref[:, cols]
        ys.append(_dot(d.astype(BF16), wg_ref[g]))
    return jnp.concatenate(ys, axis=1) * ps_ref[...]


def _fill_bias(bias_ref):
    c = lax.broadcasted_iota(jnp.int32, (3 * BLOCK, BLOCK), 0)
    a = lax.broadcasted_iota(jnp.int32, (3 * BLOCK, BLOCK), 1)
    absdist = jnp.abs(a - c + BLOCK)
    in_band = absdist <= WINDOW
    absdist_f = absdist.astype(F32)
    for h in range(N_Q_HEADS):
        slope = 2.0 ** -(h + 1)
        bias_ref[:, h * BLOCK:(h + 1) * BLOCK] = jnp.where(in_band, (-slope * LOG2E) * absdist_f, -jnp.inf)


def _attention(pos, seq, tm, sink_ref, q_ref, kv_ref, kvp_ref, kvn_ref, bias_ref, attn_t_ref):
    nblk = tm // BLOCK
    kvext = jnp.concatenate([kvp_ref[...], kv_ref[...], kvn_ref[...]], axis=0)
    lo_half = lax.broadcasted_iota(jnp.int32, (1, LANES), 1) < HEAD_DIM
    sink2 = jnp.concatenate(
        [jnp.full((1, BLOCK), sink_ref[h] * LOG2E, F32) for h in range(N_Q_HEADS)], axis=1)
    edge_first = jnp.where(pos > 0, 0.0, -jnp.inf)
    edge_last = jnp.where(pos < seq // tm - 1, 0.0, -jnp.inf)
    zero = jnp.zeros((), BF16)
    for n in range(nblk):
        qb = q_ref[n * BLOCK:(n + 1) * BLOCK, :]
        pairs = [qb[:, j * LANES:(j + 1) * LANES] for j in range(N_PAIRS)]
        qs = jnp.concatenate(
            [jnp.where(lo_half, p, zero) for p in pairs] + [jnp.where(lo_half, zero, p) for p in pairs], axis=0)
        kb = kvext[n * BLOCK:(n + 3) * BLOCK, 0:KV_WIDTH]
        vb = kvext[n * BLOCK:(n + 3) * BLOCK, KV_WIDTH:2 * KV_WIDTH]
        s = lax.dot_general(kb, qs, (((1,), (1,)), ((), ())), preferred_element_type=F32)
        s = s + bias_ref[...]
        if n == 0:
            s = jnp.concatenate([s[0:BLOCK] + edge_first, s[BLOCK:]], axis=0)
        if n == nblk - 1:
            s = jnp.concatenate([s[:2 * BLOCK], s[2 * BLOCK:] + edge_last], axis=0)
        m = jnp.maximum(jnp.max(s, axis=0, keepdims=True), sink2)
        p = jnp.exp2(s - m)
        denom = jnp.sum(p, axis=0, keepdims=True) + jnp.exp2(sink2 - m)
        o = lax.dot_general(vb, p.astype(BF16), (((0,), (0,)), ((), ())), preferred_element_type=F32)
        o = o * (1.0 / denom)
        for h in range(N_Q_HEADS):
            kvh = h // Q_GROUP
            attn_t_ref[h * HEAD_DIM:(h + 1) * HEAD_DIM, n * BLOCK:(n + 1) * BLOCK] = (
                o[kvh * HEAD_DIM:(kvh + 1) * HEAD_DIM, h * BLOCK:(h + 1) * BLOCK].astype(BF16))


def _mix_kernel(seq, tm, sink_ref, h_ref, zp_ref, zpp_ref, zpn_ref, q_ref, kv_ref, kvp_ref, kvn_ref, g_ref,
                wg_ref, ps_ref, wpb_ref, wab_ref, wo_ref, out_ref, zext_ref, bias_ref, attn_t_ref):
    @pl.when(pl.program_id(0) == 0)
    def _():
        _fill_bias(bias_ref)

    pos = pl.program_id(0) % (seq // tm)
    y = _pool_branch(pos, seq, tm, zp_ref, zpp_ref, zpn_ref, wg_ref, ps_ref, zext_ref)
    y_pool = _dot(y.astype(BF16), wpb_ref[...])
    _attention(pos, seq, tm, sink_ref, q_ref, kv_ref, kvp_ref, kvn_ref, bias_ref, attn_t_ref)
    y_attn = lax.dot_general(attn_t_ref[...], wab_ref[...], (((0,), (0,)), ((), ())), preferred_element_type=F32)
    merged = (jax.nn.sigmoid(g_ref[:, 0:D_MODEL]) * y_pool
              + jax.nn.sigmoid(g_ref[:, D_MODEL:GATE_WIDTH]) * y_attn)
    out_ref[...] = h_ref[...] + _dot(merged.astype(BF16), wo_ref[...])


def _mix_call(h, zp, q, kv, gates, sink, wg, ps, wpb, wab, wo, seq, tm):
    n = h.shape[0]
    row = lambda i: (i, 0)
    const2 = lambda i: (0, 0)
    const3 = lambda i: (0, 0, 0)
    zp_per = tm // POOL_HALO
    kv_per = tm // BLOCK
    prev_zp = lambda i: (jnp.maximum(i * zp_per - 1, 0), 0)
    next_zp = lambda i: (jnp.minimum((i + 1) * zp_per, n // POOL_HALO - 1), 0)
    prev_kv = lambda i: (jnp.maximum(i * kv_per - 1, 0), 0)
    next_kv = lambda i: (jnp.minimum((i + 1) * kv_per, n // BLOCK - 1), 0)
    return pl.pallas_call(
        functools.partial(_mix_kernel, seq, tm),
        grid=(n // tm,),
        in_specs=[
            pl.BlockSpec(memory_space=pltpu.SMEM),
            pl.BlockSpec((tm, D_MODEL), row),
            pl.BlockSpec((tm, POOL_WIDTH), row),
            pl.BlockSpec((POOL_HALO, POOL_WIDTH), prev_zp),
            pl.BlockSpec((POOL_HALO, POOL_WIDTH), next_zp),
            pl.BlockSpec((tm, Q_WIDTH), row),
            pl.BlockSpec((tm, 2 * KV_WIDTH), row),
            pl.BlockSpec((BLOCK, 2 * KV_WIDTH), prev_kv),
            pl.BlockSpec((BLOCK, 2 * KV_WIDTH), next_kv),
            pl.BlockSpec((tm, GATE_WIDTH), row),
            pl.BlockSpec((len(POOL_WINDOWS), POOL_GROUP, POOL_GROUP), const3),
            pl.BlockSpec((1, POOL_WIDTH), const2),
            pl.BlockSpec((POOL_WIDTH, D_MODEL), const2),
            pl.BlockSpec((Q_WIDTH, D_MODEL), const2),
            pl.BlockSpec((D_MODEL, D_MODEL), const2),
        ],
        out_specs=pl.BlockSpec((tm, D_MODEL), row),
        out_shape=jax.ShapeDtypeStruct((n, D_MODEL), F32),
        scratch_shapes=[
            pltpu.VMEM((tm + 2 * POOL_HALO, POOL_WIDTH), F32),
            pltpu.VMEM((3 * BLOCK, N_Q_HEADS * BLOCK), F32),
            pltpu.VMEM((Q_WIDTH, tm), BF16),
        ],
        compiler_params=pltpu.CompilerParams(
            dimension_semantics=("arbitrary",), vmem_limit_bytes=VMEM_LIMIT_BYTES),
        name="mix",
    )(sink, h, zp, zp, zp, q, kv, kv, kv, gates, wg, ps, wpb, wab, wo)


def _ffn_kernel(final, h_ref, gain_ref, wg_ref, wu_ref, wd_ref, fgain_ref, out_ref):
    h = h_ref[...]
    u = _rms_norm(h, gain_ref[...]).astype(BF16)
    gate = _dot(u, wg_ref[...])
    up = _dot(u, wu_ref[...])
    act = (jax.nn.silu(gate) * up).astype(BF16)
    h = h + _dot(act, wd_ref[...])
    out_ref[...] = _rms_norm(h, fgain_ref[...]) if final else h


def _ffn_call(h, gain, wg, wu, wd, fgain, final, tm):
    n = h.shape[0]
    d_ff = wg.shape[1]
    row = lambda i: (i, 0)
    const = lambda i: (0, 0)
    resident = dict(pipeline_mode=pl.Buffered(1))
    return pl.pallas_call(
        functools.partial(_ffn_kernel, final),
        grid=(n // tm,),
        in_specs=[
            pl.BlockSpec((tm, D_MODEL), row),
            pl.BlockSpec((1, D_MODEL), const),
            pl.BlockSpec((D_MODEL, d_ff), const, **resident),
            pl.BlockSpec((D_MODEL, d_ff), const, **resident),
            pl.BlockSpec((d_ff, D_MODEL), const, **resident),
            pl.BlockSpec((1, D_MODEL), const),
        ],
        out_specs=pl.BlockSpec((tm, D_MODEL), row),
        out_shape=jax.ShapeDtypeStruct((n, D_MODEL), F32),
        compiler_params=pltpu.CompilerParams(
            dimension_semantics=("arbitrary",), vmem_limit_bytes=VMEM_LIMIT_BYTES),
        name="ffn",
    )(h, gain, wg, wu, wd, fgain)


def _pair_heads(w, axis):
    shape = w.shape
    w = w.reshape(shape[:axis] + (N_KV_HEADS, Q_GROUP, HEAD_DIM) + shape[axis + 1:])
    return jnp.swapaxes(w, axis, axis + 1).reshape(shape)


def kernel(x, norm_mix, w_in, w_pool_group, pool_scale, sink, w_pool_branch, w_attn_branch, w_out,
           norm_ffn, w_ffn_gate, w_ffn_up, w_ffn_down, norm_final):
    batch, seq, _ = x.shape
    depth = w_in.shape[0]
    tm = 512
    assert seq % tm == 0 and tm % BLOCK == 0

    q_lo, q_hi = POOL_WIDTH, POOL_WIDTH + Q_WIDTH
    w_in_b = jnp.concatenate(
        [w_in[:, :, :q_lo], _pair_heads(w_in[:, :, q_lo:q_hi], 2), w_in[:, :, q_hi:]], axis=2).astype(BF16)

    h = x.reshape(batch * seq, D_MODEL)
    for l in range(depth):
        zp, q, kv, gates = _proj_call(h, norm_mix[l][None, :], w_in_b[l], tm)
        h = _mix_call(h, zp, q, kv, gates, sink[l], w_pool_group[l].astype(BF16), pool_scale[l][None, :],
                      w_pool_branch[l].astype(BF16), w_attn_branch[l].astype(BF16), w_out[l].astype(BF16), seq, tm)
        h = _ffn_call(h, norm_ffn[l][None, :], w_ffn_gate[l].astype(BF16), w_ffn_up[l].astype(BF16),
                      w_ffn_down[l].astype(BF16), norm_final[None, :], l == depth - 1, tm)
    return h.reshape(batch, seq, D_MODEL)
```

```python
import functools

import jax
import jax.numpy as jnp
from jax import lax
from jax.experimental import pallas as pl
from jax.experimental.pallas import tpu as pltpu

D_MODEL = 1024
POOL_WIDTH = 512
POOL_WINDOWS = (2, 4, 8, 16)
POOL_GROUP = POOL_WIDTH // len(POOL_WINDOWS)
N_Q_HEADS = 8
N_KV_HEADS = 2
HEAD_DIM = 64
Q_GROUP = N_Q_HEADS // N_KV_HEADS
Q_WIDTH = N_Q_HEADS * HEAD_DIM
KV_WIDTH = N_KV_HEADS * HEAD_DIM
WINDOW = 128
BLOCK = 128
GATE_WIDTH = 2 * D_MODEL
IN_WIDTH = POOL_WIDTH + Q_WIDTH + 2 * KV_WIDTH + GATE_WIDTH
EPS = 1e-6
LOG2E = 1.4426950408889634

LANES = 128
POOL_HALO = 8
N_PAIRS = Q_WIDTH // LANES
VMEM_LIMIT_BYTES = 56 * 1024 * 1024

F32 = jnp.float32
BF16 = jnp.bfloat16


def _rms_norm(x, gain):
    ms = jnp.mean(x * x, axis=-1, keepdims=True)
    return x * lax.rsqrt(ms + EPS) * gain


def _dot(a, b):
    return jnp.dot(a, b, preferred_element_type=F32)


def _proj_kernel(x_ref, gain_ref, w_ref, zp_ref, q_ref, kv_ref, g_ref):
    u = _rms_norm(x_ref[...], gain_ref[...]).astype(BF16)
    o = 0
    zp_ref[...] = _dot(u, w_ref[:, o:o + POOL_WIDTH])
    o += POOL_WIDTH
    q_ref[...] = (_dot(u, w_ref[:, o:o + Q_WIDTH]) * (LOG2E * HEAD_DIM ** -0.5)).astype(BF16)
    o += Q_WIDTH
    kv_ref[...] = _dot(u, w_ref[:, o:o + 2 * KV_WIDTH]).astype(BF16)
    o += 2 * KV_WIDTH
    g_ref[...] = 0.5 * _dot(u, w_ref[:, o:o + GATE_WIDTH])


def _proj_call(h, gain, w_in, tm):
    n = h.shape[0]
    row = lambda i: (i, 0)
    const = lambda i: (0, 0)
    return pl.pallas_call(
        _proj_kernel,
        grid=(n // tm,),
        in_specs=[
            pl.BlockSpec((tm, D_MODEL), row),
            pl.BlockSpec((1, D_MODEL), const),
            pl.BlockSpec((D_MODEL, IN_WIDTH), const),
        ],
        out_specs=[
            pl.BlockSpec((tm, POOL_WIDTH), row),
            pl.BlockSpec((tm, Q_WIDTH), row),
            pl.BlockSpec((tm, 2 * KV_WIDTH), row),
            pl.BlockSpec((tm, GATE_WIDTH), row),
        ],
        out_shape=[
            jax.ShapeDtypeStruct((n, POOL_WIDTH), F32),
            jax.ShapeDtypeStruct((n, Q_WIDTH), BF16),
            jax.ShapeDtypeStruct((n, 2 * KV_WIDTH), BF16),
            jax.ShapeDtypeStruct((n, GATE_WIDTH), F32),
        ],
        compiler_params=pltpu.CompilerParams(
            dimension_semantics=("arbitrary",), vmem_limit_bytes=VMEM_LIMIT_BYTES),
        name="proj",
    )(h, gain, w_in)


def _pool_branch(pos, seq, tm, zp_ref, zpp_ref, zpn_ref, wg_ref, ps_ref, zext_ref):
    blocks_per_seq = seq // tm
    h8 = POOL_HALO
    zext_ref[0:h8, :] = jnp.where(pos > 0, zpp_ref[...], 0.0)
    zext_ref[h8:h8 + tm, :] = zp_ref[...]
    zext_ref[h8 + tm:2 * h8 + tm, :] = jnp.where(pos < blocks_per_seq - 1, zpn_ref[...], 0.0)
    zext_ref[2 * h8 + tm:, :] = jnp.zeros((2 * h8, POOL_WIDTH), F32)
    r = lax.broadcasted_iota(jnp.int32, (h8, 1), 0)
    t_first = r
    t_last = seq - h8 + r
    ys = []
    for g, w in enumerate(POOL_WINDOWS):
        cols = slice(g * POOL_GROUP, (g + 1) * POOL_GROUP)
        half = w // 2
        if w == 2:
            s = zext_ref[h8 - 1:h8 - 1 + tm, cols] + zext_ref[h8:h8 + tm, cols]
        else:
            p2 = zext_ref[0:tm + 3 * h8, cols] + zext_ref[1:tm + 3 * h8 + 1, cols]
            if w == 4:
                s = p2[h8 - 2:h8 - 2 + tm] + p2[h8:h8 + tm]
            else:
                p4 = p2[0:tm + 2 * h8] + p2[2:tm + 2 * h8 + 2]
                if w == 8:
                    s = p4[h8 - 4:h8 - 4 + tm] + p4[h8:h8 + tm]
                else:
                    p8 = p4[0:tm + h8] + p4[4:tm + h8 + 4]
                    s = p8[0:tm] + p8[h8:h8 + tm]
        def inv_count(t, at_edge):
            count = (jnp.minimum(t + half, seq) - jnp.maximum(t - half, 0)).astype(F32)
            return jnp.where(at_edge, 1.0 / count, 1.0 / w)

        z = zp_ref[:, cols]
        d = jnp.concatenate([
            s[0:h8] * inv_count(t_first, pos == 0) - z[0:h8],
            s[h8:tm - h8] * (1.0 / w) - z[h8:tm - h8],
            s[tm - h8:] * inv_count(t_last, pos == blocks_per_seq - 1) - z[tm - h8:]], axis=0)
        ys.append(_dot(d.astype(BF16), wg_ref[g]))
    return jnp.concatenate(ys, axis=1) * ps_ref[...]


def _fill_bias(bias_ref):
    c = lax.broadcasted_iota(jnp.int32, (3 * BLOCK, BLOCK), 0)
    a = lax.broadcasted_iota(jnp.int32, (3 * BLOCK, BLOCK), 1)
    absdist = jnp.abs(a - c + BLOCK)
    in_band = absdist <= WINDOW
    absdist_f = absdist.astype(F32)
    for h in range(N_Q_HEADS):
        slope = 2.0 ** -(h + 1)
        bias_ref[:, h * BLOCK:(h + 1) * BLOCK] = jnp.where(in_band, (-slope * LOG2E) * absdist_f, -jnp.inf)


def _attention(pos, seq, tm, sink_ref, q_ref, kv_ref, kvp_ref, kvn_ref, bias_ref, attn_t_ref):
    nblk = tm // BLOCK
    kvext = jnp.concatenate([kvp_ref[...], kv_ref[...], kvn_ref[...]], axis=0)
    lo_half = lax.broadcasted_iota(jnp.int32, (1, LANES), 1) < HEAD_DIM
    sink2 = jnp.concatenate(
        [jnp.full((1, BLOCK), sink_ref[h] * LOG2E, F32) for h in range(N_Q_HEADS)], axis=1)
    edge_first = jnp.where(pos > 0, 0.0, -jnp.inf)
    edge_last = jnp.where(pos < seq // tm - 1, 0.0, -jnp.inf)
    zero = jnp.zeros((), BF16)
    for n in range(nblk):
        qb = q_ref[n * BLOCK:(n + 1) * BLOCK, :]
        pairs = [qb[:, j * LANES:(j + 1) * LANES] for j in range(N_PAIRS)]
        qs = jnp.concatenate(
            [jnp.where(lo_half, p, zero) for p in pairs] + [jnp.where(lo_half, zero, p) for p in pairs], axis=0)
        kb = kvext[n * BLOCK:(n + 3) * BLOCK, 0:KV_WIDTH]
        vb = kvext[n * BLOCK:(n + 3) * BLOCK, KV_WIDTH:2 * KV_WIDTH]
        s = lax.dot_general(kb, qs, (((1,), (1,)), ((), ())), preferred_element_type=F32)
        s = s + bias_ref[...]
        if n == 0:
            s = jnp.concatenate([s[0:BLOCK] + edge_first, s[BLOCK:]], axis=0)
        if n == nblk - 1:
            s = jnp.concatenate([s[:2 * BLOCK], s[2 * BLOCK:] + edge_last], axis=0)
        m = jnp.maximum(jnp.max(s, axis=0, keepdims=True), sink2)
        p = jnp.exp2(s - m)
        denom = jnp.sum(p, axis=0, keepdims=True) + jnp.exp2(sink2 - m)
        o = lax.dot_general(vb, p.astype(BF16), (((0,), (0,)), ((), ())), preferred_element_type=F32)
        o = o * (1.0 / denom)
        for h in range(N_Q_HEADS):
            kvh = h // Q_GROUP
            attn_t_ref[h * HEAD_DIM:(h + 1) * HEAD_DIM, n * BLOCK:(n + 1) * BLOCK] = (
                o[kvh * HEAD_DIM:(kvh + 1) * HEAD_DIM, h * BLOCK:(h + 1) * BLOCK].astype(BF16))


def _mix_kernel(seq, tm, sink_ref, h_ref, zp_ref, zpp_ref, zpn_ref, q_ref, kv_ref, kvp_ref, kvn_ref, g_ref,
                wg_ref, ps_ref, wpb_ref, wab_ref, wo_ref, out_ref, zext_ref, bias_ref, attn_t_ref):
    @pl.when(pl.program_id(0) == 0)
    def _():
        _fill_bias(bias_ref)

    pos = pl.program_id(0) % (seq // tm)
    y = _pool_branch(pos, seq, tm, zp_ref, zpp_ref, zpn_ref, wg_ref, ps_ref, zext_ref)
    y_pool = _dot(y.astype(BF16), wpb_ref[...])
    _attention(pos, seq, tm, sink_ref, q_ref, kv_ref, kvp_ref, kvn_ref, bias_ref, attn_t_ref)
    y_attn = lax.dot_general(attn_t_ref[...], wab_ref[...], (((0,), (0,)), ((), ())), preferred_element_type=F32)
    t_pool = jnp.tanh(g_ref[:, 0:D_MODEL])
    t_attn = jnp.tanh(g_ref[:, D_MODEL:GATE_WIDTH])
    merged = 0.5 * ((y_pool + y_attn) + (t_pool * y_pool + t_attn * y_attn))
    out_ref[...] = h_ref[...] + _dot(merged.astype(BF16), wo_ref[...])


def _mix_call(h, zp, q, kv, gates, sink, wg, ps, wpb, wab, wo, seq, tm):
    n = h.shape[0]
    row = lambda i: (i, 0)
    const2 = lambda i: (0, 0)
    const3 = lambda i: (0, 0, 0)
    zp_per = tm // POOL_HALO
    kv_per = tm // BLOCK
    prev_zp = lambda i: (jnp.maximum(i * zp_per - 1, 0), 0)
    next_zp = lambda i: (jnp.minimum((i + 1) * zp_per, n // POOL_HALO - 1), 0)
    prev_kv = lambda i: (jnp.maximum(i * kv_per - 1, 0), 0)
    next_kv = lambda i: (jnp.minimum((i + 1) * kv_per, n // BLOCK - 1), 0)
    return pl.pallas_call(
        functools.partial(_mix_kernel, seq, tm),
        grid=(n // tm,),
        in_specs=[
            pl.BlockSpec(memory_space=pltpu.SMEM),
            pl.BlockSpec((tm, D_MODEL), row),
            pl.BlockSpec((tm, POOL_WIDTH), row),
            pl.BlockSpec((POOL_HALO, POOL_WIDTH), prev_zp),
            pl.BlockSpec((POOL_HALO, POOL_WIDTH), next_zp),
            pl.BlockSpec((tm, Q_WIDTH), row),
            pl.BlockSpec((tm, 2 * KV_WIDTH), row),
            pl.BlockSpec((BLOCK, 2 * KV_WIDTH), prev_kv),
            pl.BlockSpec((BLOCK, 2 * KV_WIDTH), next_kv),
            pl.BlockSpec((tm, GATE_WIDTH), row),
            pl.BlockSpec((len(POOL_WINDOWS), POOL_GROUP, POOL_GROUP), const3),
            pl.BlockSpec((1, POOL_WIDTH), const2),
            pl.BlockSpec((POOL_WIDTH, D_MODEL), const2),
            pl.BlockSpec((Q_WIDTH, D_MODEL), const2),
            pl.BlockSpec((D_MODEL, D_MODEL), const2),
        ],
        out_specs=pl.BlockSpec((tm, D_MODEL), row),
        out_shape=jax.ShapeDtypeStruct((n, D_MODEL), F32),
        scratch_shapes=[
            pltpu.VMEM((tm + 4 * POOL_HALO, POOL_WIDTH), F32),
            pltpu.VMEM((3 * BLOCK, N_Q_HEADS * BLOCK), F32),
            pltpu.VMEM((Q_WIDTH, tm), BF16),
        ],
        compiler_params=pltpu.CompilerParams(
            dimension_semantics=("arbitrary",), vmem_limit_bytes=VMEM_LIMIT_BYTES),
        name="mix",
    )(sink, h, zp, zp, zp, q, kv, kv, kv, gates, wg, ps, wpb, wab, wo)


def _ffn_kernel(final, h_ref, gain_ref, wg_ref, wu_ref, wd_ref, fgain_ref, out_ref):
    h = h_ref[...]
    u = _rms_norm(h, gain_ref[...]).astype(BF16)
    gate = _dot(u, wg_ref[...])
    up = _dot(u, wu_ref[...])
    act = (jax.nn.silu(gate) * up).astype(BF16)
    h = h + _dot(act, wd_ref[...])
    out_ref[...] = _rms_norm(h, fgain_ref[...]) if final else h


def _ffn_call(h, gain, wg, wu, wd, fgain, final, tm):
    n = h.shape[0]
    d_ff = wg.shape[1]
    row = lambda i: (i, 0)
    const = lambda i: (0, 0)
    resident = dict(pipeline_mode=pl.Buffered(1))
    return pl.pallas_call(
        functools.partial(_ffn_kernel, final),
        grid=(n // tm,),
        in_specs=[
            pl.BlockSpec((tm, D_MODEL), row),
            pl.BlockSpec((1, D_MODEL), const),
            pl.BlockSpec((D_MODEL, d_ff), const, **resident),
            pl.BlockSpec((D_MODEL, d_ff), const, **resident),
            pl.BlockSpec((d_ff, D_MODEL), const, **resident),
            pl.BlockSpec((1, D_MODEL), const),
        ],
        out_specs=pl.BlockSpec((tm, D_MODEL), row),
        out_shape=jax.ShapeDtypeStruct((n, D_MODEL), F32),
        compiler_params=pltpu.CompilerParams(
            dimension_semantics=("arbitrary",), vmem_limit_bytes=VMEM_LIMIT_BYTES),
        name="ffn",
    )(h, gain, wg, wu, wd, fgain)


def _pair_heads(w, axis):
    shape = w.shape
    w = w.reshape(shape[:axis] + (N_KV_HEADS, Q_GROUP, HEAD_DIM) + shape[axis + 1:])
    return jnp.swapaxes(w, axis, axis + 1).reshape(shape)


def kernel(x, norm_mix, w_in, w_pool_group, pool_scale, sink, w_pool_branch, w_attn_branch, w_out,
           norm_ffn, w_ffn_gate, w_ffn_up, w_ffn_down, norm_final):
    batch, seq, _ = x.shape
    depth = w_in.shape[0]
    tm = 512
    assert seq % tm == 0 and tm % BLOCK == 0

    q_lo, q_hi = POOL_WIDTH, POOL_WIDTH + Q_WIDTH
    w_in_b = jnp.concatenate(
        [w_in[:, :, :q_lo], _pair_heads(w_in[:, :, q_lo:q_hi], 2), w_in[:, :, q_hi:]], axis=2).astype(BF16)

    h = x.reshape(batch * seq, D_MODEL)
    for l in range(depth):
        zp, q, kv, gates = _proj_call(h, norm_mix[l][None, :], w_in_b[l], tm)
        h = _mix_call(h, zp, q, kv, gates, sink[l], w_pool_group[l].astype(BF16), pool_scale[l][None, :],
                      w_pool_branch[l].astype(BF16), w_attn_branch[l].astype(BF16), w_out[l].astype(BF16), seq, tm)
        h = _ffn_call(h, norm_ffn[l][None, :], w_ffn_gate[l].astype(BF16), w_ffn_up[l].astype(BF16),
                      w_ffn_down[l].astype(BF16), norm_final[None, :], l == depth - 1, tm)
    return h.reshape(batch, seq, D_MODEL)
```

```python
import functools

import jax
import jax.numpy as jnp
from jax import lax
from jax.experimental import pallas as pl
from jax.experimental.pallas import tpu as pltpu

D_MODEL = 1024
POOL_WIDTH = 512
POOL_WINDOWS = (2, 4, 8, 16)
POOL_GROUP = POOL_WIDTH // len(POOL_WINDOWS)
N_Q_HEADS = 8
N_KV_HEADS = 2
HEAD_DIM = 64
Q_GROUP = N_Q_HEADS // N_KV_HEADS
Q_WIDTH = N_Q_HEADS * HEAD_DIM
KV_WIDTH = N_KV_HEADS * HEAD_DIM
WINDOW = 128
BLOCK = 128
GATE_WIDTH = 2 * D_MODEL
IN_WIDTH = POOL_WIDTH + Q_WIDTH + 2 * KV_WIDTH + GATE_WIDTH
EPS = 1e-6
LOG2E = 1.4426950408889634

Q_OFF = POOL_WIDTH
KV_OFF = Q_OFF + Q_WIDTH
GATE_OFF = KV_OFF + 2 * KV_WIDTH

LANES = 128
POOL_HALO = 8
BF16_ROWS = 16
N_PAIRS = Q_WIDTH // LANES
VMEM_LIMIT_BYTES = 56 * 1024 * 1024

F32 = jnp.float32
BF16 = jnp.bfloat16


def _rms_norm(x, gain):
    ms = jnp.mean(x * x, axis=-1, keepdims=True)
    return x * lax.rsqrt(ms + EPS) * gain


def _dot(a, b):
    return jnp.dot(a, b, preferred_element_type=F32)


def _dot_tn(a, b):
    return lax.dot_general(a, b, (((0,), (0,)), ((), ())), preferred_element_type=F32)


def _dot_nt(a, b):
    return lax.dot_general(a, b, (((1,), (1,)), ((), ())), preferred_element_type=F32)


def _pool_branch(pos, seq, tm, zext_ref, wg_ref, ps_ref):
    blocks_per_seq = seq // tm
    h8 = POOL_HALO
    r = lax.broadcasted_iota(jnp.int32, (h8, 1), 0)
    t_first = r
    t_last = seq - h8 + r
    ys = []
    for g, w in enumerate(POOL_WINDOWS):
        cols = slice(g * POOL_GROUP, (g + 1) * POOL_GROUP)
        half = w // 2
        if w == 2:
            s = zext_ref[h8 - 1:h8 - 1 + tm, cols] + zext_ref[h8:h8 + tm, cols]
        else:
            p2 = zext_ref[0:tm + 3 * h8, cols] + zext_ref[1:tm + 3 * h8 + 1, cols]
            if w == 4:
                s = p2[h8 - 2:h8 - 2 + tm] + p2[h8:h8 + tm]
            else:
                p4 = p2[0:tm + 2 * h8] + p2[2:tm + 2 * h8 + 2]
                if w == 8:
                    s = p4[h8 - 4:h8 - 4 + tm] + p4[h8:h8 + tm]
                else:
                    p8 = p4[0:tm + h8] + p4[4:tm + h8 + 4]
                    s = p8[0:tm] + p8[h8:h8 + tm]

        def inv_count(t, at_edge):
            count = (jnp.minimum(t + half, seq) - jnp.maximum(t - half, 0)).astype(F32)
            return jnp.where(at_edge, 1.0 / count, 1.0 / w)

        z = zext_ref[h8:h8 + tm, cols]
        d = jnp.concatenate([
            s[0:h8] * inv_count(t_first, pos == 0) - z[0:h8],
            s[h8:tm - h8] * (1.0 / w) - z[h8:tm - h8],
            s[tm - h8:] * inv_count(t_last, pos == blocks_per_seq - 1) - z[tm - h8:]], axis=0)
        ys.append(_dot(d.astype(BF16), wg_ref[g]))
    return jnp.concatenate(ys, axis=1) * ps_ref[...]


def _fill_bias(bias_ref):
    c = lax.broadcasted_iota(jnp.int32, (3 * BLOCK, BLOCK), 0)
    a = lax.broadcasted_iota(jnp.int32, (3 * BLOCK, BLOCK), 1)
    absdist = jnp.abs(a - c + BLOCK)
    in_band = absdist <= WINDOW
    absdist_f = absdist.astype(F32)
    for h in range(N_Q_HEADS):
        slope = 2.0 ** -(h + 1)
        bias_ref[:, h * BLOCK:(h + 1) * BLOCK] = jnp.where(in_band, (-slope * LOG2E) * absdist_f, -jnp.inf)


def _attention(pos, seq, tm, sink_ref, q, kvext, bias_ref, attn_t_ref):
    nblk = tm // BLOCK
    lo_half = lax.broadcasted_iota(jnp.int32, (1, LANES), 1) < HEAD_DIM
    sink2 = jnp.concatenate(
        [jnp.full((1, BLOCK), sink_ref[h] * LOG2E, F32) for h in range(N_Q_HEADS)], axis=1)
    edge_first = jnp.where(pos > 0, 0.0, -jnp.inf)
    edge_last = jnp.where(pos < seq // tm - 1, 0.0, -jnp.inf)
    zero = jnp.zeros((), BF16)
    for n in range(nblk):
        qb = q[n * BLOCK:(n + 1) * BLOCK, :]
        pairs = [qb[:, j * LANES:(j + 1) * LANES] for j in range(N_PAIRS)]
        qs = jnp.concatenate(
            [jnp.where(lo_half, p, zero) for p in pairs] + [jnp.where(lo_half, zero, p) for p in pairs], axis=0)
        kb = kvext[n * BLOCK:(n + 3) * BLOCK, 0:KV_WIDTH]
        vb = kvext[n * BLOCK:(n + 3) * BLOCK, KV_WIDTH:2 * KV_WIDTH]
        s = _dot_nt(kb, qs) + bias_ref[...]
        if n == 0:
            s = jnp.concatenate([s[0:BLOCK] + edge_first, s[BLOCK:]], axis=0)
        if n == nblk - 1:
            s = jnp.concatenate([s[:2 * BLOCK], s[2 * BLOCK:] + edge_last], axis=0)
        m = jnp.maximum(jnp.max(s, axis=0, keepdims=True), sink2)
        p = jnp.exp2(s - m)
        denom = jnp.sum(p, axis=0, keepdims=True) + jnp.exp2(sink2 - m)
        o = _dot_tn(vb, p.astype(BF16)) * (1.0 / denom)
        for h in range(N_Q_HEADS):
            kvh = h // Q_GROUP
            attn_t_ref[h * HEAD_DIM:(h + 1) * HEAD_DIM, n * BLOCK:(n + 1) * BLOCK] = (
                o[kvh * HEAD_DIM:(kvh + 1) * HEAD_DIM, h * BLOCK:(h + 1) * BLOCK].astype(BF16))


def _mix_kernel(seq, tm, sink_ref, h_ref, hp_ref, hn_ref, gain_ref, win_ref, wg_ref, ps_ref, wpb_ref, wab_ref,
                wo_ref, out_ref, zext_ref, bias_ref, attn_t_ref):
    @pl.when(pl.program_id(0) == 0)
    def _():
        _fill_bias(bias_ref)

    blocks_per_seq = seq // tm
    pos = pl.program_id(0) % blocks_per_seq
    gain = gain_ref[...]
    u = _rms_norm(h_ref[...], gain).astype(BF16)
    u_ext = jnp.concatenate(
        [_rms_norm(hp_ref[...], gain).astype(BF16), u, _rms_norm(hn_ref[...], gain).astype(BF16)], axis=0)

    z_lo = BLOCK - BF16_ROWS
    zp = _dot(u_ext[z_lo:z_lo + tm + 2 * BF16_ROWS], win_ref[:, 0:POOL_WIDTH])
    h8 = POOL_HALO
    skip = BF16_ROWS - h8
    zext_ref[0:h8, :] = jnp.where(pos > 0, zp[skip:skip + h8], 0.0)
    zext_ref[h8:h8 + tm, :] = zp[BF16_ROWS:BF16_ROWS + tm]
    zext_ref[h8 + tm:2 * h8 + tm, :] = jnp.where(
        pos < blocks_per_seq - 1, zp[BF16_ROWS + tm:BF16_ROWS + tm + h8], 0.0)
    zext_ref[2 * h8 + tm:, :] = jnp.zeros((2 * h8, POOL_WIDTH), F32)

    kvext = _dot(u_ext, win_ref[:, KV_OFF:KV_OFF + 2 * KV_WIDTH]).astype(BF16)
    q = (_dot(u, win_ref[:, Q_OFF:Q_OFF + Q_WIDTH]) * (LOG2E * HEAD_DIM ** -0.5)).astype(BF16)

    y = _pool_branch(pos, seq, tm, zext_ref, wg_ref, ps_ref)
    y_pool = _dot(y.astype(BF16), wpb_ref[...])
    _attention(pos, seq, tm, sink_ref, q, kvext, bias_ref, attn_t_ref)
    y_attn = _dot_tn(attn_t_ref[...], wab_ref[...])
    half_g = 0.5 * _dot(u, win_ref[:, GATE_OFF:GATE_OFF + GATE_WIDTH])
    t_pool = jnp.tanh(half_g[:, 0:D_MODEL])
    t_attn = jnp.tanh(half_g[:, D_MODEL:GATE_WIDTH])
    merged = 0.5 * ((y_pool + y_attn) + (t_pool * y_pool + t_attn * y_attn))
    out_ref[...] = h_ref[...] + _dot(merged.astype(BF16), wo_ref[...])


def _mix_call(h, gain, w_in, sink, wg, ps, wpb, wab, wo, seq, tm):
    n = h.shape[0]
    row = lambda i: (i, 0)
    const2 = lambda i: (0, 0)
    const3 = lambda i: (0, 0, 0)
    per = tm // BLOCK
    prev_blk = lambda i: (jnp.maximum(i * per - 1, 0), 0)
    next_blk = lambda i: (jnp.minimum((i + 1) * per, n // BLOCK - 1), 0)
    resident = dict(pipeline_mode=pl.Buffered(1))
    return pl.pallas_call(
        functools.partial(_mix_kernel, seq, tm),
        grid=(n // tm,),
        in_specs=[
            pl.BlockSpec(memory_space=pltpu.SMEM),
            pl.BlockSpec((tm, D_MODEL), row),
            pl.BlockSpec((BLOCK, D_MODEL), prev_blk),
            pl.BlockSpec((BLOCK, D_MODEL), next_blk),
            pl.BlockSpec((1, D_MODEL), const2),
            pl.BlockSpec((D_MODEL, IN_WIDTH), const2, **resident),
            pl.BlockSpec((len(POOL_WINDOWS), POOL_GROUP, POOL_GROUP), const3, **resident),
            pl.BlockSpec((1, POOL_WIDTH), const2),
            pl.BlockSpec((POOL_WIDTH, D_MODEL), const2, **resident),
            pl.BlockSpec((Q_WIDTH, D_MODEL), const2, **resident),
            pl.BlockSpec((D_MODEL, D_MODEL), const2, **resident),
        ],
        out_specs=pl.BlockSpec((tm, D_MODEL), row),
        out_shape=jax.ShapeDtypeStruct((n, D_MODEL), F32),
        scratch_shapes=[
            pltpu.VMEM((tm + 4 * POOL_HALO, POOL_WIDTH), F32),
            pltpu.VMEM((3 * BLOCK, N_Q_HEADS * BLOCK), F32),
            pltpu.VMEM((Q_WIDTH, tm), BF16),
        ],
        compiler_params=pltpu.CompilerParams(
            dimension_semantics=("arbitrary",), vmem_limit_bytes=VMEM_LIMIT_BYTES),
        name="mix",
    )(sink, h, h, h, gain, w_in, wg, ps, wpb, wab, wo)


def _ffn_kernel(final, h_ref, gain_ref, wg_ref, wu_ref, wd_ref, fgain_ref, out_ref):
    h = h_ref[...]
    u = _rms_norm(h, gain_ref[...]).astype(BF16)
    gate = _dot(u, wg_ref[...])
    up = _dot(u, wu_ref[...])
    act = (jax.nn.silu(gate) * up).astype(BF16)
    h = h + _dot(act, wd_ref[...])
    out_ref[...] = _rms_norm(h, fgain_ref[...]) if final else h


def _ffn_call(h, gain, wg, wu, wd, fgain, final, tm):
    n = h.shape[0]
    d_ff = wg.shape[1]
    row = lambda i: (i, 0)
    const = lambda i: (0, 0)
    resident = dict(pipeline_mode=pl.Buffered(1))
    return pl.pallas_call(
        functools.partial(_ffn_kernel, final),
        grid=(n // tm,),
        in_specs=[
            pl.BlockSpec((tm, D_MODEL), row),
            pl.BlockSpec((1, D_MODEL), const),
            pl.BlockSpec((D_MODEL, d_ff), const, **resident),
            pl.BlockSpec((D_MODEL, d_ff), const, **resident),
            pl.BlockSpec((d_ff, D_MODEL), const, **resident),
            pl.BlockSpec((1, D_MODEL), const),
        ],
        out_specs=pl.BlockSpec((tm, D_MODEL), row),
        out_shape=jax.ShapeDtypeStruct((n, D_MODEL), F32),
        compiler_params=pltpu.CompilerParams(
            dimension_semantics=("arbitrary",), vmem_limit_bytes=VMEM_LIMIT_BYTES),
        name="ffn",
    )(h, gain, wg, wu, wd, fgain)


def _pair_heads(w, axis):
    shape = w.shape
    w = w.reshape(shape[:axis] + (N_KV_HEADS, Q_GROUP, HEAD_DIM) + shape[axis + 1:])
    return jnp.swapaxes(w, axis, axis + 1).reshape(shape)


def kernel(x, norm_mix, w_in, w_pool_group, pool_scale, sink, w_pool_branch, w_attn_branch, w_out,
           norm_ffn, w_ffn_gate, w_ffn_up, w_ffn_down, norm_final):
    batch, seq, _ = x.shape
    depth = w_in.shape[0]
    tm = 512
    assert seq % tm == 0 and tm % BLOCK == 0

    w_in_b = jnp.concatenate(
        [w_in[:, :, :Q_OFF], _pair_heads(w_in[:, :, Q_OFF:KV_OFF], 2), w_in[:, :, KV_OFF:]], axis=2).astype(BF16)

    h = x.reshape(batch * seq, D_MODEL)
    for l in range(depth):
        h = _mix_call(h, norm_mix[l][None, :], w_in_b[l], sink[l], w_pool_group[l].astype(BF16),
                      pool_scale[l][None, :], w_pool_branch[l].astype(BF16), w_attn_branch[l].astype(BF16),
                      w_out[l].astype(BF16), seq, tm)
        h = _ffn_call(h, norm_ffn[l][None, :], w_ffn_gate[l].astype(BF16), w_ffn_up[l].astype(BF16),
                      w_ffn_down[l].astype(BF16), norm_final[None, :], l == depth - 1, tm)
    return h.reshape(batch, seq, D_MODEL)
```

```python
import functools

import jax
import jax.numpy as jnp
from jax import lax
from jax.experimental import pallas as pl
from jax.experimental.pallas import tpu as pltpu

D_MODEL = 1024
POOL_WIDTH = 512
POOL_WINDOWS = (2, 4, 8, 16)
POOL_GROUP = POOL_WIDTH // len(POOL_WINDOWS)
N_Q_HEADS = 8
N_KV_HEADS = 2
HEAD_DIM = 64
Q_GROUP = N_Q_HEADS // N_KV_HEADS
Q_WIDTH = N_Q_HEADS * HEAD_DIM
KV_WIDTH = N_KV_HEADS * HEAD_DIM
WINDOW = 128
BLOCK = 128
GATE_WIDTH = 2 * D_MODEL
IN_WIDTH = POOL_WIDTH + Q_WIDTH + 2 * KV_WIDTH + GATE_WIDTH
EPS = 1e-6
LOG2E = 1.4426950408889634

Q_OFF = POOL_WIDTH
KV_OFF = Q_OFF + Q_WIDTH
GATE_OFF = KV_OFF + 2 * KV_WIDTH

LANES = 128
POOL_HALO = 8
BF16_ROWS = 16
N_PAIRS = Q_WIDTH // LANES
VMEM_LIMIT_BYTES = 56 * 1024 * 1024

F32 = jnp.float32
BF16 = jnp.bfloat16


def _rms_norm(x, gain):
    ms = jnp.mean(x * x, axis=-1, keepdims=True)
    return x * lax.rsqrt(ms + EPS) * gain


def _dot(a, b):
    return jnp.dot(a, b, preferred_element_type=F32)


def _dot_tn(a, b):
    return lax.dot_general(a, b, (((0,), (0,)), ((), ())), preferred_element_type=F32)


def _dot_nt(a, b):
    return lax.dot_general(a, b, (((1,), (1,)), ((), ())), preferred_element_type=F32)


def _pool_branch(pos, seq, tm, zext_ref, wg_ref, ps_ref):
    blocks_per_seq = seq // tm
    h8 = POOL_HALO
    r = lax.broadcasted_iota(jnp.int32, (h8, 1), 0)
    t_first = r
    t_last = seq - h8 + r
    ys = []
    for g, w in enumerate(POOL_WINDOWS):
        cols = slice(g * POOL_GROUP, (g + 1) * POOL_GROUP)
        half = w // 2
        if w == 2:
            s = zext_ref[h8 - 1:h8 - 1 + tm, cols] + zext_ref[h8:h8 + tm, cols]
        else:
            p2 = zext_ref[0:tm + 3 * h8, cols] + zext_ref[1:tm + 3 * h8 + 1, cols]
            if w == 4:
                s = p2[h8 - 2:h8 - 2 + tm] + p2[h8:h8 + tm]
            else:
                p4 = p2[0:tm + 2 * h8] + p2[2:tm + 2 * h8 + 2]
                if w == 8:
                    s = p4[h8 - 4:h8 - 4 + tm] + p4[h8:h8 + tm]
                else:
                    p8 = p4[0:tm + h8] + p4[4:tm + h8 + 4]
                    s = p8[0:tm] + p8[h8:h8 + tm]

        def inv_count(t, at_edge):
            count = (jnp.minimum(t + half, seq) - jnp.maximum(t - half, 0)).astype(F32)
            return jnp.where(at_edge, 1.0 / count, 1.0 / w)

        z = zext_ref[h8:h8 + tm, cols]
        d = jnp.concatenate([
            s[0:h8] * inv_count(t_first, pos == 0) - z[0:h8],
            s[h8:tm - h8] * (1.0 / w) - z[h8:tm - h8],
            s[tm - h8:] * inv_count(t_last, pos == blocks_per_seq - 1) - z[tm - h8:]], axis=0)
        ys.append(_dot(d.astype(BF16), wg_ref[g]))
    return jnp.concatenate(ys, axis=1) * ps_ref[...]


def _fill_bias(bias_ref):
    c = lax.broadcasted_iota(jnp.int32, (3 * BLOCK, BLOCK), 0)
    a = lax.broadcasted_iota(jnp.int32, (3 * BLOCK, BLOCK), 1)
    absdist = jnp.abs(a - c + BLOCK)
    in_band = absdist <= WINDOW
    absdist_f = absdist.astype(F32)
    for h in range(N_Q_HEADS):
        slope = 2.0 ** -(h + 1)
        bias_ref[:, h * BLOCK:(h + 1) * BLOCK] = jnp.where(in_band, (-slope * LOG2E) * absdist_f, -jnp.inf)


def _attention(pos, seq, tm, layer, sink_ref, q, kvext, bias_ref, attn_t_ref):
    nblk = tm // BLOCK
    lo_half = lax.broadcasted_iota(jnp.int32, (1, LANES), 1) < HEAD_DIM
    sink2 = jnp.concatenate(
        [jnp.full((1, BLOCK), sink_ref[layer, h] * LOG2E, F32) for h in range(N_Q_HEADS)], axis=1)
    edge_first = jnp.where(pos > 0, 0.0, -jnp.inf)
    edge_last = jnp.where(pos < seq // tm - 1, 0.0, -jnp.inf)
    zero = jnp.zeros((), BF16)
    for n in range(nblk):
        qb = q[n * BLOCK:(n + 1) * BLOCK, :]
        pairs = [qb[:, j * LANES:(j + 1) * LANES] for j in range(N_PAIRS)]
        qs = jnp.concatenate(
            [jnp.where(lo_half, p, zero) for p in pairs] + [jnp.where(lo_half, zero, p) for p in pairs], axis=0)
        kb = kvext[n * BLOCK:(n + 3) * BLOCK, 0:KV_WIDTH]
        vb = kvext[n * BLOCK:(n + 3) * BLOCK, KV_WIDTH:2 * KV_WIDTH]
        s = _dot_nt(kb, qs) + bias_ref[...]
        if n == 0:
            s = jnp.concatenate([s[0:BLOCK] + edge_first, s[BLOCK:]], axis=0)
        if n == nblk - 1:
            s = jnp.concatenate([s[:2 * BLOCK], s[2 * BLOCK:] + edge_last], axis=0)
        m = jnp.maximum(jnp.max(s, axis=0, keepdims=True), sink2)
        p = jnp.exp2(s - m)
        denom = jnp.sum(p, axis=0, keepdims=True) + jnp.exp2(sink2 - m)
        o = _dot_tn(vb, p.astype(BF16)) * (1.0 / denom)
        for h in range(N_Q_HEADS):
            kvh = h // Q_GROUP
            attn_t_ref[h * HEAD_DIM:(h + 1) * HEAD_DIM, n * BLOCK:(n + 1) * BLOCK] = (
                o[kvh * HEAD_DIM:(kvh + 1) * HEAD_DIM, h * BLOCK:(h + 1) * BLOCK].astype(BF16))


def _mix_kernel(seq, tm, layer, sink_ref, h_ref, hp_ref, hn_ref, gain_ref, win_ref, wg_ref, ps_ref, wpb_ref, wab_ref,
                wo_ref, out_ref, zext_ref, bias_ref, attn_t_ref):
    @pl.when(pl.program_id(0) == 0)
    def _():
        _fill_bias(bias_ref)

    blocks_per_seq = seq // tm
    pos = pl.program_id(0) % blocks_per_seq
    gain = gain_ref[...]
    u = _rms_norm(h_ref[...], gain).astype(BF16)
    u_ext = jnp.concatenate(
        [_rms_norm(hp_ref[...], gain).astype(BF16), u, _rms_norm(hn_ref[...], gain).astype(BF16)], axis=0)

    z_lo = BLOCK - BF16_ROWS
    zp = _dot(u_ext[z_lo:z_lo + tm + 2 * BF16_ROWS], win_ref[:, 0:POOL_WIDTH])
    h8 = POOL_HALO
    skip = BF16_ROWS - h8
    zext_ref[0:h8, :] = jnp.where(pos > 0, zp[skip:skip + h8], 0.0)
    zext_ref[h8:h8 + tm, :] = zp[BF16_ROWS:BF16_ROWS + tm]
    zext_ref[h8 + tm:2 * h8 + tm, :] = jnp.where(
        pos < blocks_per_seq - 1, zp[BF16_ROWS + tm:BF16_ROWS + tm + h8], 0.0)
    zext_ref[2 * h8 + tm:, :] = jnp.zeros((2 * h8, POOL_WIDTH), F32)

    kvext = _dot(u_ext, win_ref[:, KV_OFF:KV_OFF + 2 * KV_WIDTH]).astype(BF16)
    q = (_dot(u, win_ref[:, Q_OFF:Q_OFF + Q_WIDTH]) * (LOG2E * HEAD_DIM ** -0.5)).astype(BF16)

    y = _pool_branch(pos, seq, tm, zext_ref, wg_ref, ps_ref)
    y_pool = _dot(y.astype(BF16), wpb_ref[...])
    _attention(pos, seq, tm, layer, sink_ref, q, kvext, bias_ref, attn_t_ref)
    y_attn = _dot_tn(attn_t_ref[...], wab_ref[...])
    half_g = 0.5 * _dot(u, win_ref[:, GATE_OFF:GATE_OFF + GATE_WIDTH])
    t_pool = jnp.tanh(half_g[:, 0:D_MODEL])
    t_attn = jnp.tanh(half_g[:, D_MODEL:GATE_WIDTH])
    merged = 0.5 * ((y_pool + y_attn) + (t_pool * y_pool + t_attn * y_attn))
    out_ref[...] = h_ref[...] + _dot(merged.astype(BF16), wo_ref[...])


def _layer_spec(arr, layer):
    zeros = (0,) * (arr.ndim - 1)
    return pl.BlockSpec((None,) + arr.shape[1:], lambda i: (layer,) + zeros, pipeline_mode=pl.Buffered(1))


def _mix_call(h, layer, gain, w_in, sink, wg, ps, wpb, wab, wo, seq, tm):
    n = h.shape[0]
    row = lambda i: (i, 0)
    per = tm // BLOCK
    prev_blk = lambda i: (jnp.maximum(i * per - 1, 0), 0)
    next_blk = lambda i: (jnp.minimum((i + 1) * per, n // BLOCK - 1), 0)
    return pl.pallas_call(
        functools.partial(_mix_kernel, seq, tm, layer),
        grid=(n // tm,),
        in_specs=[
            pl.BlockSpec(memory_space=pltpu.SMEM),
            pl.BlockSpec((tm, D_MODEL), row),
            pl.BlockSpec((BLOCK, D_MODEL), prev_blk),
            pl.BlockSpec((BLOCK, D_MODEL), next_blk),
            _layer_spec(gain, layer),
            _layer_spec(w_in, layer),
            _layer_spec(wg, layer),
            _layer_spec(ps, layer),
            _layer_spec(wpb, layer),
            _layer_spec(wab, layer),
            _layer_spec(wo, layer),
        ],
        out_specs=pl.BlockSpec((tm, D_MODEL), row),
        out_shape=jax.ShapeDtypeStruct((n, D_MODEL), F32),
        scratch_shapes=[
            pltpu.VMEM((tm + 4 * POOL_HALO, POOL_WIDTH), F32),
            pltpu.VMEM((3 * BLOCK, N_Q_HEADS * BLOCK), F32),
            pltpu.VMEM((Q_WIDTH, tm), BF16),
        ],
        compiler_params=pltpu.CompilerParams(
            dimension_semantics=("arbitrary",), vmem_limit_bytes=VMEM_LIMIT_BYTES),
        name="mix",
    )(sink, h, h, h, gain, w_in, wg, ps, wpb, wab, wo)


def _ffn_kernel(final, h_ref, gain_ref, wg_ref, wu_ref, wd_ref, fgain_ref, out_ref):
    h = h_ref[...]
    u = _rms_norm(h, gain_ref[...]).astype(BF16)
    gate = _dot(u, wg_ref[...])
    up = _dot(u, wu_ref[...])
    act = (jax.nn.silu(gate) * up).astype(BF16)
    h = h + _dot(act, wd_ref[...])
    out_ref[...] = _rms_norm(h, fgain_ref[...]) if final else h


def _ffn_call(h, layer, gain, wg, wu, wd, fgain, final, tm):
    n = h.shape[0]
    row = lambda i: (i, 0)
    return pl.pallas_call(
        functools.partial(_ffn_kernel, final),
        grid=(n // tm,),
        in_specs=[
            pl.BlockSpec((tm, D_MODEL), row),
            _layer_spec(gain, layer),
            _layer_spec(wg, layer),
            _layer_spec(wu, layer),
            _layer_spec(wd, layer),
            pl.BlockSpec((1, D_MODEL), lambda i: (0, 0)),
        ],
        out_specs=pl.BlockSpec((tm, D_MODEL), row),
        out_shape=jax.ShapeDtypeStruct((n, D_MODEL), F32),
        compiler_params=pltpu.CompilerParams(
            dimension_semantics=("arbitrary",), vmem_limit_bytes=VMEM_LIMIT_BYTES),
        name="ffn",
    )(h, gain, wg, wu, wd, fgain)


def _pair_heads(w, axis):
    shape = w.shape
    w = w.reshape(shape[:axis] + (N_KV_HEADS, Q_GROUP, HEAD_DIM) + shape[axis + 1:])
    return jnp.swapaxes(w, axis, axis + 1).reshape(shape)


def kernel(x, norm_mix, w_in, w_pool_group, pool_scale, sink, w_pool_branch, w_attn_branch, w_out,
           norm_ffn, w_ffn_gate, w_ffn_up, w_ffn_down, norm_final):
    batch, seq, _ = x.shape
    depth = w_in.shape[0]
    tm = 512
    assert seq % tm == 0 and tm % BLOCK == 0

    w_in_b = jnp.concatenate(
        [w_in[:, :, :Q_OFF], _pair_heads(w_in[:, :, Q_OFF:KV_OFF], 2), w_in[:, :, KV_OFF:]], axis=2).astype(BF16)

    wg_b, wpb_b, wab_b, wo_b = (w.astype(BF16) for w in (w_pool_group, w_pool_branch, w_attn_branch, w_out))
    wfg_b, wfu_b, wfd_b = (w.astype(BF16) for w in (w_ffn_gate, w_ffn_up, w_ffn_down))
    norm_mix3, pool_scale3, norm_ffn3 = (p[:, None, :] for p in (norm_mix, pool_scale, norm_ffn))

    h = x.reshape(batch * seq, D_MODEL)
    for l in range(depth):
        h = _mix_call(h, l, norm_mix3, w_in_b, sink, wg_b, pool_scale3, wpb_b, wab_b, wo_b, seq, tm)
        h = _ffn_call(h, l, norm_ffn3, wfg_b, wfu_b, wfd_b, norm_final[None, :], l == depth - 1, tm)
    return h.reshape(batch, seq, D_MODEL)
```

```python
import functools

import jax
import jax.numpy as jnp
from jax import lax
from jax.experimental import pallas as pl
from jax.experimental.pallas import tpu as pltpu

D_MODEL = 1024
POOL_WIDTH = 512
POOL_WINDOWS = (2, 4, 8, 16)
POOL_GROUP = POOL_WIDTH // len(POOL_WINDOWS)
N_Q_HEADS = 8
N_KV_HEADS = 2
HEAD_DIM = 64
Q_GROUP = N_Q_HEADS // N_KV_HEADS
Q_WIDTH = N_Q_HEADS * HEAD_DIM
KV_WIDTH = N_KV_HEADS * HEAD_DIM
WINDOW = 128
BLOCK = 128
GATE_WIDTH = 2 * D_MODEL
IN_WIDTH = POOL_WIDTH + Q_WIDTH + 2 * KV_WIDTH + GATE_WIDTH
EPS = 1e-6
LOG2E = 1.4426950408889634

Q_OFF = POOL_WIDTH
KV_OFF = Q_OFF + Q_WIDTH
GATE_OFF = KV_OFF + 2 * KV_WIDTH

LANES = 128
MXU_COLS = 256
FFN_CHUNK = 4 * MXU_COLS
POOL_HALO = 8
BF16_ROWS = 16
N_PAIRS = Q_WIDTH // LANES
VMEM_LIMIT_BYTES = 56 * 1024 * 1024

F32 = jnp.float32
BF16 = jnp.bfloat16


def _rms_norm(x, gain):
    ms = jnp.mean(x * x, axis=-1, keepdims=True)
    return x * lax.rsqrt(ms + EPS) * gain


def _dot(a, b):
    return jnp.dot(a, b, preferred_element_type=F32)


def _dot_tn(a, b):
    return lax.dot_general(a, b, (((0,), (0,)), ((), ())), preferred_element_type=F32)


def _dot_nt(a, b):
    return lax.dot_general(a, b, (((1,), (1,)), ((), ())), preferred_element_type=F32)


def _pool_branch(pos, seq, tm, zext_ref, wg_ref, ps_ref):
    blocks_per_seq = seq // tm
    h8 = POOL_HALO
    r = lax.broadcasted_iota(jnp.int32, (h8, 1), 0)
    t_first = r
    t_last = seq - h8 + r
    ys = []
    for g, w in enumerate(POOL_WINDOWS):
        cols = slice(g * POOL_GROUP, (g + 1) * POOL_GROUP)
        half = w // 2
        if w == 2:
            s = zext_ref[h8 - 1:h8 - 1 + tm, cols] + zext_ref[h8:h8 + tm, cols]
        else:
            p2 = zext_ref[0:tm + 3 * h8, cols] + zext_ref[1:tm + 3 * h8 + 1, cols]
            if w == 4:
                s = p2[h8 - 2:h8 - 2 + tm] + p2[h8:h8 + tm]
            else:
                p4 = p2[0:tm + 2 * h8] + p2[2:tm + 2 * h8 + 2]
                if w == 8:
                    s = p4[h8 - 4:h8 - 4 + tm] + p4[h8:h8 + tm]
                else:
                    p8 = p4[0:tm + h8] + p4[4:tm + h8 + 4]
                    s = p8[0:tm] + p8[h8:h8 + tm]

        def inv_count(t, at_edge):
            count = (jnp.minimum(t + half, seq) - jnp.maximum(t - half, 0)).astype(F32)
            return jnp.where(at_edge, 1.0 / count, 1.0 / w)

        z = zext_ref[h8:h8 + tm, cols]
        d = jnp.concatenate([
            s[0:h8] * inv_count(t_first, pos == 0) - z[0:h8],
            s[h8:tm - h8] * (1.0 / w) - z[h8:tm - h8],
            s[tm - h8:] * inv_count(t_last, pos == blocks_per_seq - 1) - z[tm - h8:]], axis=0)
        ys.append(_dot(d.astype(BF16), wg_ref[g]))
    return jnp.concatenate(ys, axis=1) * ps_ref[...]


def _fill_bias(bias_ref):
    c = lax.broadcasted_iota(jnp.int32, (3 * BLOCK, BLOCK), 0)
    a = lax.broadcasted_iota(jnp.int32, (3 * BLOCK, BLOCK), 1)
    absdist = jnp.abs(a - c + BLOCK)
    in_band = absdist <= WINDOW
    absdist_f = absdist.astype(F32)
    for h in range(N_Q_HEADS):
        slope = 2.0 ** -(h + 1)
        bias_ref[:, h * BLOCK:(h + 1) * BLOCK] = jnp.where(in_band, (-slope * LOG2E) * absdist_f, -jnp.inf)


def _attention(pos, seq, tm, layer, sink_ref, q, kvext, bias_ref, attn_t_ref):
    nblk = tm // BLOCK
    lo_half = lax.broadcasted_iota(jnp.int32, (1, LANES), 1) < HEAD_DIM
    sink2 = jnp.concatenate(
        [jnp.full((1, BLOCK), sink_ref[layer, h] * LOG2E, F32) for h in range(N_Q_HEADS)], axis=1)
    edge_first = jnp.where(pos > 0, 0.0, -jnp.inf)
    edge_last = jnp.where(pos < seq // tm - 1, 0.0, -jnp.inf)
    zero = jnp.zeros((), BF16)
    for n in range(nblk):
        qb = q[n * BLOCK:(n + 1) * BLOCK, :]
        pairs = [qb[:, j * LANES:(j + 1) * LANES] for j in range(N_PAIRS)]
        qs = jnp.concatenate(
            [jnp.where(lo_half, p, zero) for p in pairs] + [jnp.where(lo_half, zero, p) for p in pairs], axis=0)
        kb = kvext[n * BLOCK:(n + 3) * BLOCK, 0:KV_WIDTH]
        vb = kvext[n * BLOCK:(n + 3) * BLOCK, KV_WIDTH:2 * KV_WIDTH]
        s = _dot_nt(kb, qs) + bias_ref[...]
        if n == 0:
            s = jnp.concatenate([s[0:BLOCK] + edge_first, s[BLOCK:]], axis=0)
        if n == nblk - 1:
            s = jnp.concatenate([s[:2 * BLOCK], s[2 * BLOCK:] + edge_last], axis=0)
        m = jnp.maximum(jnp.max(s, axis=0, keepdims=True), sink2)
        p = jnp.exp2(s - m)
        denom = jnp.sum(p, axis=0, keepdims=True) + jnp.exp2(sink2 - m)
        o = _dot_tn(vb, p.astype(BF16)) * (1.0 / denom)
        for h in range(N_Q_HEADS):
            kvh = h // Q_GROUP
            attn_t_ref[h * HEAD_DIM:(h + 1) * HEAD_DIM, n * BLOCK:(n + 1) * BLOCK] = (
                o[kvh * HEAD_DIM:(kvh + 1) * HEAD_DIM, h * BLOCK:(h + 1) * BLOCK].astype(BF16))


def _mix_kernel(seq, tm, layer, sink_ref, h_ref, hp_ref, hn_ref, gain_ref, win_ref, wg_ref, ps_ref, wpb_ref, wab_ref,
                wo_ref, out_ref, zext_ref, bias_ref, attn_t_ref, merged_ref):
    @pl.when(pl.program_id(0) == 0)
    def _():
        _fill_bias(bias_ref)

    blocks_per_seq = seq // tm
    pos = pl.program_id(0) % blocks_per_seq
    gain = gain_ref[...]
    u = _rms_norm(h_ref[...], gain).astype(BF16)
    u_ext = jnp.concatenate(
        [_rms_norm(hp_ref[...], gain).astype(BF16), u, _rms_norm(hn_ref[...], gain).astype(BF16)], axis=0)

    z_lo = BLOCK - BF16_ROWS
    zp = _dot(u_ext[z_lo:z_lo + tm + 2 * BF16_ROWS], win_ref[:, 0:POOL_WIDTH])
    h8 = POOL_HALO
    skip = BF16_ROWS - h8
    zext_ref[0:h8, :] = jnp.where(pos > 0, zp[skip:skip + h8], 0.0)
    zext_ref[h8:h8 + tm, :] = zp[BF16_ROWS:BF16_ROWS + tm]
    zext_ref[h8 + tm:2 * h8 + tm, :] = jnp.where(
        pos < blocks_per_seq - 1, zp[BF16_ROWS + tm:BF16_ROWS + tm + h8], 0.0)
    zext_ref[2 * h8 + tm:, :] = jnp.zeros((2 * h8, POOL_WIDTH), F32)

    kvext = _dot(u_ext, win_ref[:, KV_OFF:KV_OFF + 2 * KV_WIDTH]).astype(BF16)
    q = (_dot(u, win_ref[:, Q_OFF:Q_OFF + Q_WIDTH]) * (LOG2E * HEAD_DIM ** -0.5)).astype(BF16)

    y = _pool_branch(pos, seq, tm, zext_ref, wg_ref, ps_ref).astype(BF16)
    _attention(pos, seq, tm, layer, sink_ref, q, kvext, bias_ref, attn_t_ref)
    attn_t = attn_t_ref[...]
    for c0 in range(0, D_MODEL, MXU_COLS):
        cols = slice(c0, c0 + MXU_COLS)
        y_pool = _dot(y, wpb_ref[:, cols])
        y_attn = _dot_tn(attn_t, wab_ref[:, cols])
        t_pool = jnp.tanh(0.5 * _dot(u, win_ref[:, GATE_OFF + c0:GATE_OFF + c0 + MXU_COLS]))
        t_attn = jnp.tanh(0.5 * _dot(u, win_ref[:, GATE_OFF + D_MODEL + c0:GATE_OFF + D_MODEL + c0 + MXU_COLS]))
        merged_ref[:, cols] = (0.5 * ((y_pool + y_attn) + (t_pool * y_pool + t_attn * y_attn))).astype(BF16)
    out_ref[...] = h_ref[...] + _dot(merged_ref[...], wo_ref[...])


def _layer_spec(arr, layer):
    zeros = (0,) * (arr.ndim - 1)
    return pl.BlockSpec((None,) + arr.shape[1:], lambda i: (layer,) + zeros, pipeline_mode=pl.Buffered(1))


def _mix_call(h, layer, gain, w_in, sink, wg, ps, wpb, wab, wo, seq, tm):
    n = h.shape[0]
    row = lambda i: (i, 0)
    per = tm // BLOCK
    prev_blk = lambda i: (jnp.maximum(i * per - 1, 0), 0)
    next_blk = lambda i: (jnp.minimum((i + 1) * per, n // BLOCK - 1), 0)
    return pl.pallas_call(
        functools.partial(_mix_kernel, seq, tm, layer),
        grid=(n // tm,),
        in_specs=[
            pl.BlockSpec(memory_space=pltpu.SMEM),
            pl.BlockSpec((tm, D_MODEL), row),
            pl.BlockSpec((BLOCK, D_MODEL), prev_blk),
            pl.BlockSpec((BLOCK, D_MODEL), next_blk),
            _layer_spec(gain, layer),
            _layer_spec(w_in, layer),
            _layer_spec(wg, layer),
            _layer_spec(ps, layer),
            _layer_spec(wpb, layer),
            _layer_spec(wab, layer),
            _layer_spec(wo, layer),
        ],
        out_specs=pl.BlockSpec((tm, D_MODEL), row),
        out_shape=jax.ShapeDtypeStruct((n, D_MODEL), F32),
        scratch_shapes=[
            pltpu.VMEM((tm + 4 * POOL_HALO, POOL_WIDTH), F32),
            pltpu.VMEM((3 * BLOCK, N_Q_HEADS * BLOCK), F32),
            pltpu.VMEM((Q_WIDTH, tm), BF16),
            pltpu.VMEM((tm, D_MODEL), BF16),
        ],
        compiler_params=pltpu.CompilerParams(
            dimension_semantics=("arbitrary",), vmem_limit_bytes=VMEM_LIMIT_BYTES),
        name="mix",
    )(sink, h, h, h, gain, w_in, wg, ps, wpb, wab, wo)


def _ffn_kernel(final, h_ref, gain_ref, wg_ref, wu_ref, wd_ref, fgain_ref, out_ref):
    h = h_ref[...]
    u = _rms_norm(h, gain_ref[...]).astype(BF16)
    d_ff = wg_ref.shape[1]
    for c0 in range(0, d_ff, FFN_CHUNK):
        cols = slice(c0, min(c0 + FFN_CHUNK, d_ff))
        act = (jax.nn.silu(_dot(u, wg_ref[:, cols])) * _dot(u, wu_ref[:, cols])).astype(BF16)
        h = h + _dot(act, wd_ref[cols, :])
    out_ref[...] = _rms_norm(h, fgain_ref[...]) if final else h


def _ffn_call(h, layer, gain, wg, wu, wd, fgain, final, tm):
    n = h.shape[0]
    row = lambda i: (i, 0)
    return pl.pallas_call(
        functools.partial(_ffn_kernel, final),
        grid=(n // tm,),
        in_specs=[
            pl.BlockSpec((tm, D_MODEL), row),
            _layer_spec(gain, layer),
            _layer_spec(wg, layer),
            _layer_spec(wu, layer),
            _layer_spec(wd, layer),
            pl.BlockSpec((1, D_MODEL), lambda i: (0, 0)),
        ],
        out_specs=pl.BlockSpec((tm, D_MODEL), row),
        out_shape=jax.ShapeDtypeStruct((n, D_MODEL), F32),
        compiler_params=pltpu.CompilerParams(
            dimension_semantics=("arbitrary",), vmem_limit_bytes=VMEM_LIMIT_BYTES),
        name="ffn",
    )(h, gain, wg, wu, wd, fgain)


def _pair_heads(w, axis):
    shape = w.shape
    w = w.reshape(shape[:axis] + (N_KV_HEADS, Q_GROUP, HEAD_DIM) + shape[axis + 1:])
    return jnp.swapaxes(w, axis, axis + 1).reshape(shape)


def kernel(x, norm_mix, w_in, w_pool_group, pool_scale, sink, w_pool_branch, w_attn_branch, w_out,
           norm_ffn, w_ffn_gate, w_ffn_up, w_ffn_down, norm_final):
    batch, seq, _ = x.shape
    depth = w_in.shape[0]
    tm_mix, tm_ffn = 1024, 1024
    assert seq % tm_mix == 0 and tm_mix % BLOCK == 0 and (batch * seq) % tm_ffn == 0

    w_in_b = jnp.concatenate(
        [w_in[:, :, :Q_OFF], _pair_heads(w_in[:, :, Q_OFF:KV_OFF], 2), w_in[:, :, KV_OFF:]], axis=2).astype(BF16)

    wg_b, wpb_b, wab_b, wo_b = (w.astype(BF16) for w in (w_pool_group, w_pool_branch, w_attn_branch, w_out))
    wfg_b, wfu_b, wfd_b = (w.astype(BF16) for w in (w_ffn_gate, w_ffn_up, w_ffn_down))
    norm_mix3, pool_scale3, norm_ffn3 = (p[:, None, :] for p in (norm_mix, pool_scale, norm_ffn))

    h = x.reshape(batch * seq, D_MODEL)
    for l in range(depth):
        h = _mix_call(h, l, norm_mix3, w_in_b, sink, wg_b, pool_scale3, wpb_b, wab_b, wo_b, seq, tm_mix)
        h = _ffn_call(h, l, norm_ffn3, wfg_b, wfu_b, wfd_b, norm_final[None, :], l == depth - 1, tm_ffn)
    return h.reshape(batch, seq, D_MODEL)
```

```python
import functools

import jax
import jax.numpy as jnp
from jax import lax
from jax.experimental import pallas as pl
from jax.experimental.pallas import tpu as pltpu

D_MODEL = 1024
POOL_WIDTH = 512
POOL_WINDOWS = (2, 4, 8, 16)
POOL_GROUP = POOL_WIDTH // len(POOL_WINDOWS)
N_Q_HEADS = 8
N_KV_HEADS = 2
HEAD_DIM = 64
Q_GROUP = N_Q_HEADS // N_KV_HEADS
Q_WIDTH = N_Q_HEADS * HEAD_DIM
KV_WIDTH = N_KV_HEADS * HEAD_DIM
WINDOW = 128
BLOCK = 128
GATE_WIDTH = 2 * D_MODEL
IN_WIDTH = POOL_WIDTH + Q_WIDTH + 2 * KV_WIDTH + GATE_WIDTH
EPS = 1e-6
LOG2E = 1.4426950408889634

Q_OFF = POOL_WIDTH
KV_OFF = Q_OFF + Q_WIDTH
GATE_OFF = KV_OFF + 2 * KV_WIDTH

LANES = 128
MXU_COLS = 256
FFN_CHUNK = 4 * MXU_COLS
POOL_HALO = 8
BF16_ROWS = 16
N_PAIRS = Q_WIDTH // LANES
VMEM_LIMIT_BYTES = 56 * 1024 * 1024

F32 = jnp.float32
BF16 = jnp.bfloat16


def _rms_norm(x, gain):
    ms = jnp.mean(x * x, axis=-1, keepdims=True)
    return x * lax.rsqrt(ms + EPS) * gain


def _dot(a, b):
    return jnp.dot(a, b, preferred_element_type=F32)


def _dot_tn(a, b):
    return lax.dot_general(a, b, (((0,), (0,)), ((), ())), preferred_element_type=F32)


def _dot_nt(a, b):
    return lax.dot_general(a, b, (((1,), (1,)), ((), ())), preferred_element_type=F32)


def _pool_branch(pos, seq, tm, zext_ref, wg_ref, ps_ref):
    blocks_per_seq = seq // tm
    h8 = POOL_HALO
    r = lax.broadcasted_iota(jnp.int32, (h8, 1), 0)
    t_first = r
    t_last = seq - h8 + r
    ys = []
    for g, w in enumerate(POOL_WINDOWS):
        cols = slice(g * POOL_GROUP, (g + 1) * POOL_GROUP)
        half = w // 2
        if w == 2:
            s = zext_ref[h8 - 1:h8 - 1 + tm, cols] + zext_ref[h8:h8 + tm, cols]
        else:
            p2 = zext_ref[0:tm + 3 * h8, cols] + zext_ref[1:tm + 3 * h8 + 1, cols]
            if w == 4:
                s = p2[h8 - 2:h8 - 2 + tm] + p2[h8:h8 + tm]
            else:
                p4 = p2[0:tm + 2 * h8] + p2[2:tm + 2 * h8 + 2]
                if w == 8:
                    s = p4[h8 - 4:h8 - 4 + tm] + p4[h8:h8 + tm]
                else:
                    p8 = p4[0:tm + h8] + p4[4:tm + h8 + 4]
                    s = p8[0:tm] + p8[h8:h8 + tm]

        def inv_count(t, at_edge):
            count = (jnp.minimum(t + half, seq) - jnp.maximum(t - half, 0)).astype(F32)
            return jnp.where(at_edge, 1.0 / count, 1.0 / w)

        z = zext_ref[h8:h8 + tm, cols]
        d = jnp.concatenate([
            s[0:h8] * inv_count(t_first, pos == 0) - z[0:h8],
            s[h8:tm - h8] * (1.0 / w) - z[h8:tm - h8],
            s[tm - h8:] * inv_count(t_last, pos == blocks_per_seq - 1) - z[tm - h8:]], axis=0)
        ys.append(_dot(d.astype(BF16), wg_ref[g]))
    return jnp.concatenate(ys, axis=1) * ps_ref[...]


def _fill_bias(bias_ref):
    c = lax.broadcasted_iota(jnp.int32, (3 * BLOCK, BLOCK), 0)
    a = lax.broadcasted_iota(jnp.int32, (3 * BLOCK, BLOCK), 1)
    absdist = jnp.abs(a - c + BLOCK)
    in_band = absdist <= WINDOW
    absdist_f = absdist.astype(F32)
    for h in range(N_Q_HEADS):
        slope = 2.0 ** -(h + 1)
        bias_ref[:, h * BLOCK:(h + 1) * BLOCK] = jnp.where(in_band, (-slope * LOG2E) * absdist_f, -jnp.inf)


def _attention(pos, seq, tm, layer, sink_ref, q, kvext, bias_ref, attn_t_ref, side_work):
    nblk = tm // BLOCK
    lo_half = lax.broadcasted_iota(jnp.int32, (1, LANES), 1) < HEAD_DIM
    sink2 = jnp.concatenate(
        [jnp.full((1, BLOCK), sink_ref[layer, h] * LOG2E, F32) for h in range(N_Q_HEADS)], axis=1)
    edge_first = jnp.where(pos > 0, 0.0, -jnp.inf)
    edge_last = jnp.where(pos < seq // tm - 1, 0.0, -jnp.inf)
    zero = jnp.zeros((), BF16)
    for n in range(nblk):
        qb = q[n * BLOCK:(n + 1) * BLOCK, :]
        pairs = [qb[:, j * LANES:(j + 1) * LANES] for j in range(N_PAIRS)]
        qs = jnp.concatenate(
            [jnp.where(lo_half, p, zero) for p in pairs] + [jnp.where(lo_half, zero, p) for p in pairs], axis=0)
        kb = kvext[n * BLOCK:(n + 3) * BLOCK, 0:KV_WIDTH]
        vb = kvext[n * BLOCK:(n + 3) * BLOCK, KV_WIDTH:2 * KV_WIDTH]
        s = _dot_nt(kb, qs) + bias_ref[...]
        if n == 0:
            s = jnp.concatenate([s[0:BLOCK] + edge_first, s[BLOCK:]], axis=0)
        if n == nblk - 1:
            s = jnp.concatenate([s[:2 * BLOCK], s[2 * BLOCK:] + edge_last], axis=0)
        for work in side_work[n * len(side_work) // nblk:(n + 1) * len(side_work) // nblk]:
            work()
        m = jnp.maximum(jnp.max(s, axis=0, keepdims=True), sink2)
        p = jnp.exp2(s - m)
        denom = jnp.sum(p, axis=0, keepdims=True) + jnp.exp2(sink2 - m)
        o = _dot_tn(vb, p.astype(BF16)) * (1.0 / denom)
        for h in range(N_Q_HEADS):
            kvh = h // Q_GROUP
            attn_t_ref[h * HEAD_DIM:(h + 1) * HEAD_DIM, n * BLOCK:(n + 1) * BLOCK] = (
                o[kvh * HEAD_DIM:(kvh + 1) * HEAD_DIM, h * BLOCK:(h + 1) * BLOCK].astype(BF16))


def _mix_kernel(seq, tm, layer, sink_ref, h_ref, hp_ref, hn_ref, gain_ref, win_ref, wg_ref, ps_ref, wpb_ref, wab_ref,
                wo_ref, out_ref, zext_ref, bias_ref, attn_t_ref, gate_ref, pooled_ref, merged_ref):
    @pl.when(pl.program_id(0) == 0)
    def _():
        _fill_bias(bias_ref)

    blocks_per_seq = seq // tm
    pos = pl.program_id(0) % blocks_per_seq
    gain = gain_ref[...]
    u = _rms_norm(h_ref[...], gain).astype(BF16)
    u_ext = jnp.concatenate(
        [_rms_norm(hp_ref[...], gain).astype(BF16), u, _rms_norm(hn_ref[...], gain).astype(BF16)], axis=0)

    z_lo = BLOCK - BF16_ROWS
    zp = _dot(u_ext[z_lo:z_lo + tm + 2 * BF16_ROWS], win_ref[:, 0:POOL_WIDTH])
    h8 = POOL_HALO
    skip = BF16_ROWS - h8
    zext_ref[0:h8, :] = jnp.where(pos > 0, zp[skip:skip + h8], 0.0)
    zext_ref[h8:h8 + tm, :] = zp[BF16_ROWS:BF16_ROWS + tm]
    zext_ref[h8 + tm:2 * h8 + tm, :] = jnp.where(
        pos < blocks_per_seq - 1, zp[BF16_ROWS + tm:BF16_ROWS + tm + h8], 0.0)
    zext_ref[2 * h8 + tm:, :] = jnp.zeros((2 * h8, POOL_WIDTH), F32)

    kvext = _dot(u_ext, win_ref[:, KV_OFF:KV_OFF + 2 * KV_WIDTH]).astype(BF16)
    q = (_dot(u, win_ref[:, Q_OFF:Q_OFF + Q_WIDTH]) * (LOG2E * HEAD_DIM ** -0.5)).astype(BF16)

    y = _pool_branch(pos, seq, tm, zext_ref, wg_ref, ps_ref).astype(BF16)

    def gate(c0):
        return 0.5 + 0.5 * jnp.tanh(0.5 * _dot(u, win_ref[:, GATE_OFF + c0:GATE_OFF + c0 + MXU_COLS]))

    def attn_gate_chunk(c0):
        gate_ref[:, c0:c0 + MXU_COLS] = gate(D_MODEL + c0)

    def pool_chunk(c0):
        cols = slice(c0, c0 + MXU_COLS)
        pooled_ref[:, cols] = gate(c0) * _dot(y, wpb_ref[:, cols])

    chunks = range(0, D_MODEL, MXU_COLS)
    side_work = ([functools.partial(attn_gate_chunk, c0) for c0 in chunks]
                 + [functools.partial(pool_chunk, c0) for c0 in chunks])
    _attention(pos, seq, tm, layer, sink_ref, q, kvext, bias_ref, attn_t_ref, side_work)
    attn_t = attn_t_ref[...]
    for c0 in chunks:
        cols = slice(c0, c0 + MXU_COLS)
        merged = pooled_ref[:, cols] + gate_ref[:, cols] * _dot_tn(attn_t, wab_ref[:, cols])
        merged_ref[:, cols] = merged.astype(BF16)
    out_ref[...] = h_ref[...] + _dot(merged_ref[...], wo_ref[...])


def _layer_spec(arr, layer):
    zeros = (0,) * (arr.ndim - 1)
    return pl.BlockSpec((None,) + arr.shape[1:], lambda i: (layer,) + zeros, pipeline_mode=pl.Buffered(1))


def _mix_call(h, layer, gain, w_in, sink, wg, ps, wpb, wab, wo, seq, tm):
    n = h.shape[0]
    row = lambda i: (i, 0)
    per = tm // BLOCK
    prev_blk = lambda i: (jnp.maximum(i * per - 1, 0), 0)
    next_blk = lambda i: (jnp.minimum((i + 1) * per, n // BLOCK - 1), 0)
    return pl.pallas_call(
        functools.partial(_mix_kernel, seq, tm, layer),
        grid=(n // tm,),
        in_specs=[
            pl.BlockSpec(memory_space=pltpu.SMEM),
            pl.BlockSpec((tm, D_MODEL), row),
            pl.BlockSpec((BLOCK, D_MODEL), prev_blk),
            pl.BlockSpec((BLOCK, D_MODEL), next_blk),
            _layer_spec(gain, layer),
            _layer_spec(w_in, layer),
            _layer_spec(wg, layer),
            _layer_spec(ps, layer),
            _layer_spec(wpb, layer),
            _layer_spec(wab, layer),
            _layer_spec(wo, layer),
        ],
        out_specs=pl.BlockSpec((tm, D_MODEL), row),
        out_shape=jax.ShapeDtypeStruct((n, D_MODEL), F32),
        scratch_shapes=[
            pltpu.VMEM((tm + 4 * POOL_HALO, POOL_WIDTH), F32),
            pltpu.VMEM((3 * BLOCK, N_Q_HEADS * BLOCK), F32),
            pltpu.VMEM((Q_WIDTH, tm), BF16),
            pltpu.VMEM((tm, D_MODEL), F32),
            pltpu.VMEM((tm, D_MODEL), F32),
            pltpu.VMEM((tm, D_MODEL), BF16),
        ],
        compiler_params=pltpu.CompilerParams(
            dimension_semantics=("arbitrary",), vmem_limit_bytes=VMEM_LIMIT_BYTES),
        name="mix",
    )(sink, h, h, h, gain, w_in, wg, ps, wpb, wab, wo)


def _ffn_kernel(final, h_ref, gain_ref, wg_ref, wu_ref, wd_ref, fgain_ref, out_ref):
    h = h_ref[...]
    u = _rms_norm(h, gain_ref[...]).astype(BF16)
    d_ff = wg_ref.shape[1]
    for c0 in range(0, d_ff, FFN_CHUNK):
        cols = slice(c0, min(c0 + FFN_CHUNK, d_ff))
        act = (jax.nn.silu(_dot(u, wg_ref[:, cols])) * _dot(u, wu_ref[:, cols])).astype(BF16)
        h = h + _dot(act, wd_ref[cols, :])
    out_ref[...] = _rms_norm(h, fgain_ref[...]) if final else h


def _ffn_call(h, layer, gain, wg, wu, wd, fgain, final, tm):
    n = h.shape[0]
    row = lambda i: (i, 0)
    return pl.pallas_call(
        functools.partial(_ffn_kernel, final),
        grid=(n // tm,),
        in_specs=[
            pl.BlockSpec((tm, D_MODEL), row),
            _layer_spec(gain, layer),
            _layer_spec(wg, layer),
            _layer_spec(wu, layer),
            _layer_spec(wd, layer),
            pl.BlockSpec((1, D_MODEL), lambda i: (0, 0)),
        ],
        out_specs=pl.BlockSpec((tm, D_MODEL), row),
        out_shape=jax.ShapeDtypeStruct((n, D_MODEL), F32),
        compiler_params=pltpu.CompilerParams(
            dimension_semantics=("arbitrary",), vmem_limit_bytes=VMEM_LIMIT_BYTES),
        name="ffn",
    )(h, gain, wg, wu, wd, fgain)


def _pair_heads(w, axis):
    shape = w.shape
    w = w.reshape(shape[:axis] + (N_KV_HEADS, Q_GROUP, HEAD_DIM) + shape[axis + 1:])
    return jnp.swapaxes(w, axis, axis + 1).reshape(shape)


def kernel(x, norm_mix, w_in, w_pool_group, pool_scale, sink, w_pool_branch, w_attn_branch, w_out,
           norm_ffn, w_ffn_gate, w_ffn_up, w_ffn_down, norm_final):
    batch, seq, _ = x.shape
    depth = w_in.shape[0]
    tm_mix, tm_ffn = 1024, 1024
    assert seq % tm_mix == 0 and tm_mix % BLOCK == 0 and (batch * seq) % tm_ffn == 0

    w_in_b = jnp.concatenate(
        [w_in[:, :, :Q_OFF], _pair_heads(w_in[:, :, Q_OFF:KV_OFF], 2), w_in[:, :, KV_OFF:]], axis=2).astype(BF16)

    wg_b, wpb_b, wab_b, wo_b = (w.astype(BF16) for w in (w_pool_group, w_pool_branch, w_attn_branch, w_out))
    wfg_b, wfu_b, wfd_b = (w.astype(BF16) for w in (w_ffn_gate, w_ffn_up, w_ffn_down))
    norm_mix3, pool_scale3, norm_ffn3 = (p[:, None, :] for p in (norm_mix, pool_scale, norm_ffn))

    h = x.reshape(batch * seq, D_MODEL)
    for l in range(depth):
        h = _mix_call(h, l, norm_mix3, w_in_b, sink, wg_b, pool_scale3, wpb_b, wab_b, wo_b, seq, tm_mix)
        h = _ffn_call(h, l, norm_ffn3, wfg_b, wfu_b, wfd_b, norm_final[None, :], l == depth - 1, tm_ffn)
    return h.reshape(batch, seq, D_MODEL)
```

```python
import functools

import jax
import jax.numpy as jnp
from jax import lax
from jax.experimental import pallas as pl
from jax.experimental.pallas import tpu as pltpu

D_MODEL = 1024
POOL_WIDTH = 512
POOL_WINDOWS = (2, 4, 8, 16)
POOL_GROUP = POOL_WIDTH // len(POOL_WINDOWS)
N_Q_HEADS = 8
N_KV_HEADS = 2
HEAD_DIM = 64
Q_GROUP = N_Q_HEADS // N_KV_HEADS
Q_WIDTH = N_Q_HEADS * HEAD_DIM
KV_WIDTH = N_KV_HEADS * HEAD_DIM
WINDOW = 128
BLOCK = 128
GATE_WIDTH = 2 * D_MODEL
IN_WIDTH = POOL_WIDTH + Q_WIDTH + 2 * KV_WIDTH + GATE_WIDTH
EPS = 1e-6
LOG2E = 1.4426950408889634

Q_OFF = POOL_WIDTH
KV_OFF = Q_OFF + Q_WIDTH
GATE_OFF = KV_OFF + 2 * KV_WIDTH

LANES = 128
MXU_COLS = 256
FFN_CHUNK = 4 * MXU_COLS
STAGE_ROWS = 128
POOL_HALO = 8
BF16_ROWS = 16
N_PAIRS = Q_WIDTH // LANES
VMEM_LIMIT_BYTES = 56 * 1024 * 1024

F32 = jnp.float32
BF16 = jnp.bfloat16


def _rms_norm(x, gain):
    ms = jnp.mean(x * x, axis=-1, keepdims=True)
    return x * lax.rsqrt(ms + EPS) * gain


def _dot(a, b):
    return jnp.dot(a, b, preferred_element_type=F32)


def _dot_tn(a, b):
    return lax.dot_general(a, b, (((0,), (0,)), ((), ())), preferred_element_type=F32)


def _dot_nt(a, b):
    return lax.dot_general(a, b, (((1,), (1,)), ((), ())), preferred_element_type=F32)


def _pool_branch(pos, seq, tm, zext_ref, wg_ref, ps_ref):
    blocks_per_seq = seq // tm
    h8 = POOL_HALO
    r = lax.broadcasted_iota(jnp.int32, (h8, 1), 0)
    t_first = r
    t_last = seq - h8 + r
    ys = []
    for g, w in enumerate(POOL_WINDOWS):
        cols = slice(g * POOL_GROUP, (g + 1) * POOL_GROUP)
        half = w // 2
        if w == 2:
            s = zext_ref[h8 - 1:h8 - 1 + tm, cols] + zext_ref[h8:h8 + tm, cols]
        else:
            p2 = zext_ref[0:tm + 3 * h8, cols] + zext_ref[1:tm + 3 * h8 + 1, cols]
            if w == 4:
                s = p2[h8 - 2:h8 - 2 + tm] + p2[h8:h8 + tm]
            else:
                p4 = p2[0:tm + 2 * h8] + p2[2:tm + 2 * h8 + 2]
                if w == 8:
                    s = p4[h8 - 4:h8 - 4 + tm] + p4[h8:h8 + tm]
                else:
                    p8 = p4[0:tm + h8] + p4[4:tm + h8 + 4]
                    s = p8[0:tm] + p8[h8:h8 + tm]

        def inv_count(t, at_edge):
            count = (jnp.minimum(t + half, seq) - jnp.maximum(t - half, 0)).astype(F32)
            return jnp.where(at_edge, 1.0 / count, 1.0 / w)

        z = zext_ref[h8:h8 + tm, cols]
        d = jnp.concatenate([
            s[0:h8] * inv_count(t_first, pos == 0) - z[0:h8],
            s[h8:tm - h8] * (1.0 / w) - z[h8:tm - h8],
            s[tm - h8:] * inv_count(t_last, pos == blocks_per_seq - 1) - z[tm - h8:]], axis=0)
        ys.append(_dot(d.astype(BF16), wg_ref[g]))
    return jnp.concatenate(ys, axis=1) * ps_ref[...]


def _fill_bias(bias_ref):
    c = lax.broadcasted_iota(jnp.int32, (3 * BLOCK, BLOCK), 0)
    a = lax.broadcasted_iota(jnp.int32, (3 * BLOCK, BLOCK), 1)
    absdist = jnp.abs(a - c + BLOCK)
    in_band = absdist <= WINDOW
    absdist_f = absdist.astype(F32)
    for h in range(N_Q_HEADS):
        slope = 2.0 ** -(h + 1)
        bias_ref[:, h * BLOCK:(h + 1) * BLOCK] = jnp.where(in_band, (-slope * LOG2E) * absdist_f, -jnp.inf)


def _attention(pos, seq, tm, layer, sink_ref, q, kvext, bias_ref, attn_t_ref, side_work):
    nblk = tm // BLOCK
    lo_half = lax.broadcasted_iota(jnp.int32, (1, LANES), 1) < HEAD_DIM
    sink2 = jnp.concatenate(
        [jnp.full((1, BLOCK), sink_ref[layer, h] * LOG2E, F32) for h in range(N_Q_HEADS)], axis=1)
    edge_first = jnp.where(pos > 0, 0.0, -jnp.inf)
    edge_last = jnp.where(pos < seq // tm - 1, 0.0, -jnp.inf)
    zero = jnp.zeros((), BF16)
    for n in range(nblk):
        qb = q[n * BLOCK:(n + 1) * BLOCK, :]
        pairs = [qb[:, j * LANES:(j + 1) * LANES] for j in range(N_PAIRS)]
        qs = jnp.concatenate(
            [jnp.where(lo_half, p, zero) for p in pairs] + [jnp.where(lo_half, zero, p) for p in pairs], axis=0)
        kb = kvext[n * BLOCK:(n + 3) * BLOCK, 0:KV_WIDTH]
        vb = kvext[n * BLOCK:(n + 3) * BLOCK, KV_WIDTH:2 * KV_WIDTH]
        s = _dot_nt(kb, qs) + bias_ref[...]
        if n == 0:
            s = jnp.concatenate([s[0:BLOCK] + edge_first, s[BLOCK:]], axis=0)
        if n == nblk - 1:
            s = jnp.concatenate([s[:2 * BLOCK], s[2 * BLOCK:] + edge_last], axis=0)
        for work in side_work[n * len(side_work) // nblk:(n + 1) * len(side_work) // nblk]:
            work()
        m = jnp.maximum(jnp.max(s, axis=0, keepdims=True), sink2)
        p = jnp.exp2(s - m)
        denom = jnp.sum(p, axis=0, keepdims=True) + jnp.exp2(sink2 - m)
        o = _dot_tn(vb, p.astype(BF16)) * (1.0 / denom)
        for h in range(N_Q_HEADS):
            kvh = h // Q_GROUP
            attn_t_ref[h * HEAD_DIM:(h + 1) * HEAD_DIM, n * BLOCK:(n + 1) * BLOCK] = (
                o[kvh * HEAD_DIM:(kvh + 1) * HEAD_DIM, h * BLOCK:(h + 1) * BLOCK].astype(BF16))


def _mix_kernel(seq, tm, layer, sink_ref, h_ref, hp_ref, hn_ref, gain_ref, win_ref, wg_ref, ps_ref, wpb_ref, wab_ref,
                wo_ref, out_ref, zext_ref, bias_ref, attn_t_ref, gate_ref, pooled_ref, merged_ref):
    @pl.when(pl.program_id(0) == 0)
    def _():
        _fill_bias(bias_ref)

    blocks_per_seq = seq // tm
    pos = pl.program_id(0) % blocks_per_seq
    gain = gain_ref[...]
    u = _rms_norm(h_ref[...], gain).astype(BF16)
    u_ext = jnp.concatenate(
        [_rms_norm(hp_ref[...], gain).astype(BF16), u, _rms_norm(hn_ref[...], gain).astype(BF16)], axis=0)

    z_lo = BLOCK - BF16_ROWS
    zp = _dot(u_ext[z_lo:z_lo + tm + 2 * BF16_ROWS], win_ref[:, 0:POOL_WIDTH])
    h8 = POOL_HALO
    skip = BF16_ROWS - h8
    zext_ref[0:h8, :] = jnp.where(pos > 0, zp[skip:skip + h8], 0.0)
    zext_ref[h8:h8 + tm, :] = zp[BF16_ROWS:BF16_ROWS + tm]
    zext_ref[h8 + tm:2 * h8 + tm, :] = jnp.where(
        pos < blocks_per_seq - 1, zp[BF16_ROWS + tm:BF16_ROWS + tm + h8], 0.0)
    zext_ref[2 * h8 + tm:, :] = jnp.zeros((2 * h8, POOL_WIDTH), F32)

    kvext = _dot(u_ext, win_ref[:, KV_OFF:KV_OFF + 2 * KV_WIDTH]).astype(BF16)
    q = (_dot(u, win_ref[:, Q_OFF:Q_OFF + Q_WIDTH]) * (LOG2E * HEAD_DIM ** -0.5)).astype(BF16)

    y = _pool_branch(pos, seq, tm, zext_ref, wg_ref, ps_ref).astype(BF16)

    def gate(c0):
        return 0.5 + 0.5 * jnp.tanh(0.5 * _dot(u, win_ref[:, GATE_OFF + c0:GATE_OFF + c0 + MXU_COLS]))

    def attn_gate_chunk(c0):
        gate_ref[:, c0:c0 + MXU_COLS] = gate(D_MODEL + c0)

    def pool_chunk(c0):
        cols = slice(c0, c0 + MXU_COLS)
        pooled_ref[:, cols] = gate(c0) * _dot(y, wpb_ref[:, cols])

    chunks = range(0, D_MODEL, MXU_COLS)
    side_work = ([functools.partial(attn_gate_chunk, c0) for c0 in chunks]
                 + [functools.partial(pool_chunk, c0) for c0 in chunks])
    _attention(pos, seq, tm, layer, sink_ref, q, kvext, bias_ref, attn_t_ref, side_work)
    attn_t = attn_t_ref[...]
    for c0 in chunks:
        cols = slice(c0, c0 + MXU_COLS)
        merged = pooled_ref[:, cols] + gate_ref[:, cols] * _dot_tn(attn_t, wab_ref[:, cols])
        merged_ref[:, cols] = merged.astype(BF16)
    out_ref[...] = h_ref[...] + _dot(merged_ref[...], wo_ref[...])


def _layer_spec(arr, layer):
    zeros = (0,) * (arr.ndim - 1)
    return pl.BlockSpec((None,) + arr.shape[1:], lambda i: (layer,) + zeros, pipeline_mode=pl.Buffered(1))


def _mix_call(h, layer, gain, w_in, sink, wg, ps, wpb, wab, wo, seq, tm):
    n = h.shape[0]
    row = lambda i: (i, 0)
    per = tm // BLOCK
    prev_blk = lambda i: (jnp.maximum(i * per - 1, 0), 0)
    next_blk = lambda i: (jnp.minimum((i + 1) * per, n // BLOCK - 1), 0)
    return pl.pallas_call(
        functools.partial(_mix_kernel, seq, tm, layer),
        grid=(n // tm,),
        in_specs=[
            pl.BlockSpec(memory_space=pltpu.SMEM),
            pl.BlockSpec((tm, D_MODEL), row),
            pl.BlockSpec((BLOCK, D_MODEL), prev_blk),
            pl.BlockSpec((BLOCK, D_MODEL), next_blk),
            _layer_spec(gain, layer),
            _layer_spec(w_in, layer),
            _layer_spec(wg, layer),
            _layer_spec(ps, layer),
            _layer_spec(wpb, layer),
            _layer_spec(wab, layer),
            _layer_spec(wo, layer),
        ],
        out_specs=pl.BlockSpec((tm, D_MODEL), row),
        out_shape=jax.ShapeDtypeStruct((n, D_MODEL), F32),
        scratch_shapes=[
            pltpu.VMEM((tm + 4 * POOL_HALO, POOL_WIDTH), F32),
            pltpu.VMEM((3 * BLOCK, N_Q_HEADS * BLOCK), F32),
            pltpu.VMEM((Q_WIDTH, tm), BF16),
            pltpu.VMEM((tm, D_MODEL), F32),
            pltpu.VMEM((tm, D_MODEL), F32),
            pltpu.VMEM((tm, D_MODEL), BF16),
        ],
        compiler_params=pltpu.CompilerParams(
            dimension_semantics=("arbitrary",), vmem_limit_bytes=VMEM_LIMIT_BYTES),
        name="mix",
    )(sink, h, h, h, gain, w_in, wg, ps, wpb, wab, wo)


def _stage_weight(src_hbm, layer, dst_ref, stage_ref, sem):
    rows, cols = dst_ref.shape
    n_chunks = rows // STAGE_ROWS
    assert rows % STAGE_ROWS == 0 and cols <= stage_ref.shape[2]

    def copy(k, slot):
        return pltpu.make_async_copy(
            src_hbm.at[layer, pl.ds(k * STAGE_ROWS, STAGE_ROWS), :],
            stage_ref.at[slot, :, pl.ds(0, cols)],
            sem.at[slot])

    copy(0, 0).start()

    def body(k, carry):
        slot = k % 2

        @pl.when(k + 1 < n_chunks)
        def _():
            copy(k + 1, 1 - slot).start()

        copy(k, slot).wait()
        dst_ref[pl.ds(pl.multiple_of(k * STAGE_ROWS, STAGE_ROWS), STAGE_ROWS), :] = (
            stage_ref[slot, :, 0:cols].astype(BF16))
        return carry

    lax.fori_loop(0, n_chunks, body, 0)


def _ffn_kernel(final, layer, h_ref, gain_ref, wg_hbm, wu_hbm, wd_hbm, fgain_ref, out_ref,
                wg_ref, wu_ref, wd_ref, stage_ref, sem):
    @pl.when(pl.program_id(0) == 0)
    def _():
        _stage_weight(wg_hbm, layer, wg_ref, stage_ref, sem)
        _stage_weight(wu_hbm, layer, wu_ref, stage_ref, sem)
        _stage_weight(wd_hbm, layer, wd_ref, stage_ref, sem)

    h = h_ref[...]
    u = _rms_norm(h, gain_ref[...]).astype(BF16)
    d_ff = wg_ref.shape[1]
    for c0 in range(0, d_ff, FFN_CHUNK):
        cols = slice(c0, min(c0 + FFN_CHUNK, d_ff))
        act = (jax.nn.silu(_dot(u, wg_ref[:, cols])) * _dot(u, wu_ref[:, cols])).astype(BF16)
        h = h + _dot(act, wd_ref[cols, :])
    out_ref[...] = _rms_norm(h, fgain_ref[...]) if final else h


def _ffn_call(h, layer, gain, wg, wu, wd, fgain, final, tm):
    n = h.shape[0]
    d_ff = wg.shape[2]
    row = lambda i: (i, 0)
    hbm = pl.BlockSpec(memory_space=pl.ANY)
    return pl.pallas_call(
        functools.partial(_ffn_kernel, final, layer),
        grid=(n // tm,),
        in_specs=[
            pl.BlockSpec((tm, D_MODEL), row),
            _layer_spec(gain, layer),
            hbm, hbm, hbm,
            pl.BlockSpec((1, D_MODEL), lambda i: (0, 0)),
        ],
        out_specs=pl.BlockSpec((tm, D_MODEL), row),
        out_shape=jax.ShapeDtypeStruct((n, D_MODEL), F32),
        scratch_shapes=[
            pltpu.VMEM((D_MODEL, d_ff), BF16),
            pltpu.VMEM((D_MODEL, d_ff), BF16),
            pltpu.VMEM((d_ff, D_MODEL), BF16),
            pltpu.VMEM((2, STAGE_ROWS, d_ff), F32),
            pltpu.SemaphoreType.DMA((2,)),
        ],
        compiler_params=pltpu.CompilerParams(
            dimension_semantics=("arbitrary",), vmem_limit_bytes=VMEM_LIMIT_BYTES),
        name="ffn",
    )(h, gain, wg, wu, wd, fgain)


def _pair_heads(w, axis):
    shape = w.shape
    w = w.reshape(shape[:axis] + (N_KV_HEADS, Q_GROUP, HEAD_DIM) + shape[axis + 1:])
    return jnp.swapaxes(w, axis, axis + 1).reshape(shape)


def kernel(x, norm_mix, w_in, w_pool_group, pool_scale, sink, w_pool_branch, w_attn_branch, w_out,
           norm_ffn, w_ffn_gate, w_ffn_up, w_ffn_down, norm_final):
    batch, seq, _ = x.shape
    depth = w_in.shape[0]
    tm_mix, tm_ffn = 1024, 1024
    assert seq % tm_mix == 0 and tm_mix % BLOCK == 0 and (batch * seq) % tm_ffn == 0

    w_in_b = jnp.concatenate(
        [w_in[:, :, :Q_OFF], _pair_heads(w_in[:, :, Q_OFF:KV_OFF], 2), w_in[:, :, KV_OFF:]], axis=2).astype(BF16)

    wg_b, wpb_b, wab_b, wo_b = (w.astype(BF16) for w in (w_pool_group, w_pool_branch, w_attn_branch, w_out))
    norm_mix3, pool_scale3, norm_ffn3 = (p[:, None, :] for p in (norm_mix, pool_scale, norm_ffn))

    h = x.reshape(batch * seq, D_MODEL)
    for l in range(depth):
        h = _mix_call(h, l, norm_mix3, w_in_b, sink, wg_b, pool_scale3, wpb_b, wab_b, wo_b, seq, tm_mix)
        h = _ffn_call(h, l, norm_ffn3, w_ffn_gate, w_ffn_up, w_ffn_down, norm_final[None, :], l == depth - 1,
                      tm_ffn)
    return h.reshape(batch, seq, D_MODEL)
```

```python
import functools

import jax
import jax.numpy as jnp
from jax import lax
from jax.experimental import pallas as pl
from jax.experimental.pallas import tpu as pltpu

D_MODEL = 1024
POOL_WIDTH = 512
POOL_WINDOWS = (2, 4, 8, 16)
POOL_GROUP = POOL_WIDTH // len(POOL_WINDOWS)
N_Q_HEADS = 8
N_KV_HEADS = 2
HEAD_DIM = 64
Q_GROUP = N_Q_HEADS // N_KV_HEADS
Q_WIDTH = N_Q_HEADS * HEAD_DIM
KV_WIDTH = N_KV_HEADS * HEAD_DIM
WINDOW = 128
BLOCK = 128
GATE_WIDTH = 2 * D_MODEL
IN_WIDTH = POOL_WIDTH + Q_WIDTH + 2 * KV_WIDTH + GATE_WIDTH
EPS = 1e-6
LOG2E = 1.4426950408889634

Q_OFF = POOL_WIDTH
KV_OFF = Q_OFF + Q_WIDTH
GATE_OFF = KV_OFF + 2 * KV_WIDTH

LANES = 128
MXU_COLS = 256
FFN_CHUNK = 4 * MXU_COLS
STAGE_ROWS = 128
STAGE_WINDOWS = 4
POOL_HALO = 8
BF16_ROWS = 16
N_PAIRS = Q_WIDTH // LANES
VMEM_LIMIT_BYTES = 56 * 1024 * 1024

F32 = jnp.float32
BF16 = jnp.bfloat16


def _rms_norm(x, gain):
    ms = jnp.mean(x * x, axis=-1, keepdims=True)
    return x * lax.rsqrt(ms + EPS) * gain


def _dot(a, b):
    return jnp.dot(a, b, preferred_element_type=F32)


def _dot_tn(a, b):
    return lax.dot_general(a, b, (((0,), (0,)), ((), ())), preferred_element_type=F32)


def _dot_nt(a, b):
    return lax.dot_general(a, b, (((1,), (1,)), ((), ())), preferred_element_type=F32)


def _pool_branch(pos, seq, tm, zext_ref, wg_ref, ps_ref):
    blocks_per_seq = seq // tm
    h8 = POOL_HALO
    r = lax.broadcasted_iota(jnp.int32, (h8, 1), 0)
    t_first = r
    t_last = seq - h8 + r
    ys = []
    for g, w in enumerate(POOL_WINDOWS):
        cols = slice(g * POOL_GROUP, (g + 1) * POOL_GROUP)
        half = w // 2
        if w == 2:
            s = zext_ref[h8 - 1:h8 - 1 + tm, cols] + zext_ref[h8:h8 + tm, cols]
        else:
            p2 = zext_ref[0:tm + 3 * h8, cols] + zext_ref[1:tm + 3 * h8 + 1, cols]
            if w == 4:
                s = p2[h8 - 2:h8 - 2 + tm] + p2[h8:h8 + tm]
            else:
                p4 = p2[0:tm + 2 * h8] + p2[2:tm + 2 * h8 + 2]
                if w == 8:
                    s = p4[h8 - 4:h8 - 4 + tm] + p4[h8:h8 + tm]
                else:
                    p8 = p4[0:tm + h8] + p4[4:tm + h8 + 4]
                    s = p8[0:tm] + p8[h8:h8 + tm]

        def inv_count(t, at_edge):
            count = (jnp.minimum(t + half, seq) - jnp.maximum(t - half, 0)).astype(F32)
            return jnp.where(at_edge, 1.0 / count, 1.0 / w)

        z = zext_ref[h8:h8 + tm, cols]
        d = jnp.concatenate([
            s[0:h8] * inv_count(t_first, pos == 0) - z[0:h8],
            s[h8:tm - h8] * (1.0 / w) - z[h8:tm - h8],
            s[tm - h8:] * inv_count(t_last, pos == blocks_per_seq - 1) - z[tm - h8:]], axis=0)
        ys.append(_dot(d.astype(BF16), wg_ref[g]))
    return jnp.concatenate(ys, axis=1) * ps_ref[...]


def _fill_bias(bias_ref):
    c = lax.broadcasted_iota(jnp.int32, (3 * BLOCK, BLOCK), 0)
    a = lax.broadcasted_iota(jnp.int32, (3 * BLOCK, BLOCK), 1)
    absdist = jnp.abs(a - c + BLOCK)
    in_band = absdist <= WINDOW
    absdist_f = absdist.astype(F32)
    for h in range(N_Q_HEADS):
        slope = 2.0 ** -(h + 1)
        bias_ref[:, h * BLOCK:(h + 1) * BLOCK] = jnp.where(in_band, (-slope * LOG2E) * absdist_f, -jnp.inf)


def _attention(pos, seq, tm, layer, sink_ref, q, kvext, bias_ref, attn_t_ref, side_work):
    nblk = tm // BLOCK
    lo_half = lax.broadcasted_iota(jnp.int32, (1, LANES), 1) < HEAD_DIM
    sink2 = jnp.concatenate(
        [jnp.full((1, BLOCK), sink_ref[layer, h] * LOG2E, F32) for h in range(N_Q_HEADS)], axis=1)
    edge_first = jnp.where(pos > 0, 0.0, -jnp.inf)
    edge_last = jnp.where(pos < seq // tm - 1, 0.0, -jnp.inf)
    zero = jnp.zeros((), BF16)
    for n in range(nblk):
        qb = q[n * BLOCK:(n + 1) * BLOCK, :]
        pairs = [qb[:, j * LANES:(j + 1) * LANES] for j in range(N_PAIRS)]
        qs = jnp.concatenate(
            [jnp.where(lo_half, p, zero) for p in pairs] + [jnp.where(lo_half, zero, p) for p in pairs], axis=0)
        kb = kvext[n * BLOCK:(n + 3) * BLOCK, 0:KV_WIDTH]
        vb = kvext[n * BLOCK:(n + 3) * BLOCK, KV_WIDTH:2 * KV_WIDTH]
        s = _dot_nt(kb, qs) + bias_ref[...]
        if n == 0:
            s = jnp.concatenate([s[0:BLOCK] + edge_first, s[BLOCK:]], axis=0)
        if n == nblk - 1:
            s = jnp.concatenate([s[:2 * BLOCK], s[2 * BLOCK:] + edge_last], axis=0)
        for work in side_work[n * len(side_work) // nblk:(n + 1) * len(side_work) // nblk]:
            work()
        m = jnp.maximum(jnp.max(s, axis=0, keepdims=True), sink2)
        p = jnp.exp2(s - m)
        denom = jnp.sum(p, axis=0, keepdims=True) + jnp.exp2(sink2 - m)
        o = _dot_tn(vb, p.astype(BF16)) * (1.0 / denom)
        for h in range(N_Q_HEADS):
            kvh = h // Q_GROUP
            attn_t_ref[h * HEAD_DIM:(h + 1) * HEAD_DIM, n * BLOCK:(n + 1) * BLOCK] = (
                o[kvh * HEAD_DIM:(kvh + 1) * HEAD_DIM, h * BLOCK:(h + 1) * BLOCK].astype(BF16))


def _mix_kernel(seq, tm, layer, sink_ref, h_ref, hp_ref, hn_ref, gain_ref, win_ref, wg_ref, ps_ref, wpb_ref, wab_ref,
                wo_ref, out_ref, zext_ref, bias_ref, attn_t_ref, gate_ref, pooled_ref, merged_ref):
    @pl.when(pl.program_id(0) == 0)
    def _():
        _fill_bias(bias_ref)

    blocks_per_seq = seq // tm
    pos = pl.program_id(0) % blocks_per_seq
    gain = gain_ref[...]
    u = _rms_norm(h_ref[...], gain).astype(BF16)
    u_ext = jnp.concatenate(
        [_rms_norm(hp_ref[...], gain).astype(BF16), u, _rms_norm(hn_ref[...], gain).astype(BF16)], axis=0)

    z_lo = BLOCK - BF16_ROWS
    zp = _dot(u_ext[z_lo:z_lo + tm + 2 * BF16_ROWS], win_ref[:, 0:POOL_WIDTH])
    h8 = POOL_HALO
    skip = BF16_ROWS - h8
    zext_ref[0:h8, :] = jnp.where(pos > 0, zp[skip:skip + h8], 0.0)
    zext_ref[h8:h8 + tm, :] = zp[BF16_ROWS:BF16_ROWS + tm]
    zext_ref[h8 + tm:2 * h8 + tm, :] = jnp.where(
        pos < blocks_per_seq - 1, zp[BF16_ROWS + tm:BF16_ROWS + tm + h8], 0.0)
    zext_ref[2 * h8 + tm:, :] = jnp.zeros((2 * h8, POOL_WIDTH), F32)

    kvext = _dot(u_ext, win_ref[:, KV_OFF:KV_OFF + 2 * KV_WIDTH]).astype(BF16)
    q = (_dot(u, win_ref[:, Q_OFF:Q_OFF + Q_WIDTH]) * (LOG2E * HEAD_DIM ** -0.5)).astype(BF16)

    y = _pool_branch(pos, seq, tm, zext_ref, wg_ref, ps_ref).astype(BF16)

    def gate(c0):
        return 0.5 + 0.5 * jnp.tanh(0.5 * _dot(u, win_ref[:, GATE_OFF + c0:GATE_OFF + c0 + MXU_COLS]))

    def attn_gate_chunk(c0):
        gate_ref[:, c0:c0 + MXU_COLS] = gate(D_MODEL + c0)

    def pool_chunk(c0):
        cols = slice(c0, c0 + MXU_COLS)
        pooled_ref[:, cols] = gate(c0) * _dot(y, wpb_ref[:, cols])

    chunks = range(0, D_MODEL, MXU_COLS)
    side_work = ([functools.partial(attn_gate_chunk, c0) for c0 in chunks]
                 + [functools.partial(pool_chunk, c0) for c0 in chunks])
    _attention(pos, seq, tm, layer, sink_ref, q, kvext, bias_ref, attn_t_ref, side_work)
    attn_t = attn_t_ref[...]
    for c0 in chunks:
        cols = slice(c0, c0 + MXU_COLS)
        merged = pooled_ref[:, cols] + gate_ref[:, cols] * _dot_tn(attn_t, wab_ref[:, cols])
        merged_ref[:, cols] = merged.astype(BF16)
    out_ref[...] = h_ref[...] + _dot(merged_ref[...], wo_ref[...])


def _layer_spec(arr, layer):
    zeros = (0,) * (arr.ndim - 1)
    return pl.BlockSpec((None,) + arr.shape[1:], lambda i: (layer,) + zeros, pipeline_mode=pl.Buffered(1))


def _mix_call(h, layer, gain, w_in, sink, wg, ps, wpb, wab, wo, seq, tm):
    n = h.shape[0]
    row = lambda i: (i, 0)
    per = tm // BLOCK
    prev_blk = lambda i: (jnp.maximum(i * per - 1, 0), 0)
    next_blk = lambda i: (jnp.minimum((i + 1) * per, n // BLOCK - 1), 0)
    return pl.pallas_call(
        functools.partial(_mix_kernel, seq, tm, layer),
        grid=(n // tm,),
        in_specs=[
            pl.BlockSpec(memory_space=pltpu.SMEM),
            pl.BlockSpec((tm, D_MODEL), row),
            pl.BlockSpec((BLOCK, D_MODEL), prev_blk),
            pl.BlockSpec((BLOCK, D_MODEL), next_blk),
            _layer_spec(gain, layer),
            _layer_spec(w_in, layer),
            _layer_spec(wg, layer),
            _layer_spec(ps, layer),
            _layer_spec(wpb, layer),
            _layer_spec(wab, layer),
            _layer_spec(wo, layer),
        ],
        out_specs=pl.BlockSpec((tm, D_MODEL), row),
        out_shape=jax.ShapeDtypeStruct((n, D_MODEL), F32),
        scratch_shapes=[
            pltpu.VMEM((tm + 4 * POOL_HALO, POOL_WIDTH), F32),
            pltpu.VMEM((3 * BLOCK, N_Q_HEADS * BLOCK), F32),
            pltpu.VMEM((Q_WIDTH, tm), BF16),
            pltpu.VMEM((tm, D_MODEL), F32),
            pltpu.VMEM((tm, D_MODEL), F32),
            pltpu.VMEM((tm, D_MODEL), BF16),
        ],
        compiler_params=pltpu.CompilerParams(
            dimension_semantics=("arbitrary",), vmem_limit_bytes=VMEM_LIMIT_BYTES),
        name="mix",
    )(sink, h, h, h, gain, w_in, wg, ps, wpb, wab, wo)


def _stage_weights(pairs, layer, stage_ref, sem):
    n_win, win_rows, win_cols = stage_ref.shape
    chunks = []
    for src, dst in pairs:
        rows, cols = dst.shape
        assert rows % win_rows == 0 and cols <= win_cols
        chunks += [(src, dst, r0, cols) for r0 in range(0, rows, win_rows)]

    def copy(j):
        src, _, r0, cols = chunks[j]
        win = j % n_win
        return pltpu.make_async_copy(
            src.at[layer, pl.ds(r0, win_rows), :], stage_ref.at[win, :, pl.ds(0, cols)], sem.at[win])

    for j in range(min(n_win, len(chunks))):
        copy(j).start()
    for j, (_, dst, r0, cols) in enumerate(chunks):
        copy(j).wait()
        dst[r0:r0 + win_rows, :] = stage_ref[j % n_win, :, 0:cols].astype(BF16)
        if j + n_win < len(chunks):
            copy(j + n_win).start()


def _ffn_kernel(final, layer, h_ref, gain_ref, wg_hbm, wu_hbm, wd_hbm, fgain_ref, out_ref,
                wg_ref, wu_ref, wd_ref, stage_ref, sem):
    @pl.when(pl.program_id(0) == 0)
    def _():
        _stage_weights([(wg_hbm, wg_ref), (wu_hbm, wu_ref), (wd_hbm, wd_ref)], layer, stage_ref, sem)

    h = h_ref[...]
    u = _rms_norm(h, gain_ref[...]).astype(BF16)
    d_ff = wg_ref.shape[1]
    for c0 in range(0, d_ff, FFN_CHUNK):
        cols = slice(c0, min(c0 + FFN_CHUNK, d_ff))
        act = (jax.nn.silu(_dot(u, wg_ref[:, cols])) * _dot(u, wu_ref[:, cols])).astype(BF16)
        h = h + _dot(act, wd_ref[cols, :])
    out_ref[...] = _rms_norm(h, fgain_ref[...]) if final else h


def _ffn_call(h, layer, gain, wg, wu, wd, fgain, final, tm):
    n = h.shape[0]
    d_ff = wg.shape[2]
    row = lambda i: (i, 0)
    hbm = pl.BlockSpec(memory_space=pl.ANY)
    return pl.pallas_call(
        functools.partial(_ffn_kernel, final, layer),
        grid=(n // tm,),
        in_specs=[
            pl.BlockSpec((tm, D_MODEL), row),
            _layer_spec(gain, layer),
            hbm, hbm, hbm,
            pl.BlockSpec((1, D_MODEL), lambda i: (0, 0)),
        ],
        out_specs=pl.BlockSpec((tm, D_MODEL), row),
        out_shape=jax.ShapeDtypeStruct((n, D_MODEL), F32),
        scratch_shapes=[
            pltpu.VMEM((D_MODEL, d_ff), BF16),
            pltpu.VMEM((D_MODEL, d_ff), BF16),
            pltpu.VMEM((d_ff, D_MODEL), BF16),
            pltpu.VMEM((STAGE_WINDOWS, STAGE_ROWS, d_ff), F32),
            pltpu.SemaphoreType.DMA((STAGE_WINDOWS,)),
        ],
        compiler_params=pltpu.CompilerParams(
            dimension_semantics=("arbitrary",), vmem_limit_bytes=VMEM_LIMIT_BYTES),
        name="ffn",
    )(h, gain, wg, wu, wd, fgain)


def _pair_heads(w, axis):
    shape = w.shape
    w = w.reshape(shape[:axis] + (N_KV_HEADS, Q_GROUP, HEAD_DIM) + shape[axis + 1:])
    return jnp.swapaxes(w, axis, axis + 1).reshape(shape)


def kernel(x, norm_mix, w_in, w_pool_group, pool_scale, sink, w_pool_branch, w_attn_branch, w_out,
           norm_ffn, w_ffn_gate, w_ffn_up, w_ffn_down, norm_final):
    batch, seq, _ = x.shape
    depth = w_in.shape[0]
    tm_mix, tm_ffn = 1024, 1024
    assert seq % tm_mix == 0 and tm_mix % BLOCK == 0 and (batch * seq) % tm_ffn == 0

    w_in_b = jnp.concatenate(
        [w_in[:, :, :Q_OFF], _pair_heads(w_in[:, :, Q_OFF:KV_OFF], 2), w_in[:, :, KV_OFF:]], axis=2).astype(BF16)

    wg_b, wpb_b, wab_b, wo_b = (w.astype(BF16) for w in (w_pool_group, w_pool_branch, w_attn_branch, w_out))
    norm_mix3, pool_scale3, norm_ffn3 = (p[:, None, :] for p in (norm_mix, pool_scale, norm_ffn))

    h = x.reshape(batch * seq, D_MODEL)
    for l in range(depth):
        h = _mix_call(h, l, norm_mix3, w_in_b, sink, wg_b, pool_scale3, wpb_b, wab_b, wo_b, seq, tm_mix)
        h = _ffn_call(h, l, norm_ffn3, w_ffn_gate, w_ffn_up, w_ffn_down, norm_final[None, :], l == depth - 1,
                      tm_ffn)
    return h.reshape(batch, seq, D_MODEL)
```

```python
import functools

import jax
import jax.numpy as jnp
from jax import lax
from jax.experimental import pallas as pl
from jax.experimental.pallas import tpu as pltpu

D_MODEL = 1024
POOL_WIDTH = 512
POOL_WINDOWS = (2, 4, 8, 16)
POOL_GROUP = POOL_WIDTH // len(POOL_WINDOWS)
N_Q_HEADS = 8
N_KV_HEADS = 2
HEAD_DIM = 64
Q_GROUP = N_Q_HEADS // N_KV_HEADS
Q_WIDTH = N_Q_HEADS * HEAD_DIM
KV_WIDTH = N_KV_HEADS * HEAD_DIM
WINDOW = 128
BLOCK = 128
GATE_WIDTH = 2 * D_MODEL
IN_WIDTH = POOL_WIDTH + Q_WIDTH + 2 * KV_WIDTH + GATE_WIDTH
EPS = 1e-6
LOG2E = 1.4426950408889634

Q_OFF = POOL_WIDTH
KV_OFF = Q_OFF + Q_WIDTH
GATE_OFF = KV_OFF + 2 * KV_WIDTH

LANES = 128
MXU_COLS = 256
FFN_CHUNK = 4 * MXU_COLS
STAGE_ROWS = 128
STAGE_WINDOWS = 4
POOL_HALO = 8
BF16_ROWS = 16
assert LANES == 2 * HEAD_DIM and N_KV_HEADS == 2
ALIGNED_HEADS = tuple(h for h in range(N_Q_HEADS) if h % 2 == h // Q_GROUP)
HEAD_ORDER = ALIGNED_HEADS + tuple(h for h in range(N_Q_HEADS) if h not in ALIGNED_HEADS)
VMEM_LIMIT_BYTES = 56 * 1024 * 1024

F32 = jnp.float32
BF16 = jnp.bfloat16


def _rms_norm(x, gain):
    ms = jnp.mean(x * x, axis=-1, keepdims=True)
    return x * lax.rsqrt(ms + EPS) * gain


def _dot(a, b):
    return jnp.dot(a, b, preferred_element_type=F32)


def _dot_tn(a, b):
    return lax.dot_general(a, b, (((0,), (0,)), ((), ())), preferred_element_type=F32)


def _dot_nt(a, b):
    return lax.dot_general(a, b, (((1,), (1,)), ((), ())), preferred_element_type=F32)


def _pool_branch(pos, seq, tm, zext_ref, wg_ref, ps_ref):
    blocks_per_seq = seq // tm
    h8 = POOL_HALO
    r = lax.broadcasted_iota(jnp.int32, (h8, 1), 0)
    t_first = r
    t_last = seq - h8 + r
    ys = []
    for g, w in enumerate(POOL_WINDOWS):
        cols = slice(g * POOL_GROUP, (g + 1) * POOL_GROUP)
        half = w // 2
        if w == 2:
            s = zext_ref[h8 - 1:h8 - 1 + tm, cols] + zext_ref[h8:h8 + tm, cols]
        else:
            p2 = zext_ref[0:tm + 3 * h8, cols] + zext_ref[1:tm + 3 * h8 + 1, cols]
            if w == 4:
                s = p2[h8 - 2:h8 - 2 + tm] + p2[h8:h8 + tm]
            else:
                p4 = p2[0:tm + 2 * h8] + p2[2:tm + 2 * h8 + 2]
                if w == 8:
                    s = p4[h8 - 4:h8 - 4 + tm] + p4[h8:h8 + tm]
                else:
                    p8 = p4[0:tm + h8] + p4[4:tm + h8 + 4]
                    s = p8[0:tm] + p8[h8:h8 + tm]

        def inv_count(t, at_edge):
            count = (jnp.minimum(t + half, seq) - jnp.maximum(t - half, 0)).astype(F32)
            return jnp.where(at_edge, 1.0 / count, 1.0 / w)

        z = zext_ref[h8:h8 + tm, cols]
        d = jnp.concatenate([
            s[0:h8] * inv_count(t_first, pos == 0) - z[0:h8],
            s[h8:tm - h8] * (1.0 / w) - z[h8:tm - h8],
            s[tm - h8:] * inv_count(t_last, pos == blocks_per_seq - 1) - z[tm - h8:]], axis=0)
        ys.append(_dot(d.astype(BF16), wg_ref[g]))
    return jnp.concatenate(ys, axis=1) * ps_ref[...]


def _fill_bias(bias_ref):
    c = lax.broadcasted_iota(jnp.int32, (3 * BLOCK, BLOCK), 0)
    a = lax.broadcasted_iota(jnp.int32, (3 * BLOCK, BLOCK), 1)
    absdist = jnp.abs(a - c + BLOCK)
    in_band = absdist <= WINDOW
    absdist_f = absdist.astype(F32)
    for col, h in enumerate(HEAD_ORDER):
        slope = 2.0 ** -(h + 1)
        bias_ref[:, col * BLOCK:(col + 1) * BLOCK] = jnp.where(in_band, (-slope * LOG2E) * absdist_f, -jnp.inf)


def _attention(pos, seq, tm, layer, sink_ref, q, k, k_swapped, v, bias_ref, attn_t_ref, side_work):
    nblk = tm // BLOCK
    lane_half = lax.broadcasted_iota(jnp.int32, (1, LANES), 1) // HEAD_DIM
    sink2 = jnp.concatenate(
        [jnp.full((1, BLOCK), sink_ref[layer, h] * LOG2E, F32) for h in HEAD_ORDER], axis=1)
    edge_first = jnp.where(pos > 0, 0.0, -jnp.inf)
    edge_last = jnp.where(pos < seq // tm - 1, 0.0, -jnp.inf)
    zero = jnp.zeros((), BF16)
    n_aligned = len(ALIGNED_HEADS)
    for n in range(nblk):
        qb = q[n * BLOCK:(n + 1) * BLOCK, :]

        def head_rows(h):
            block = qb[:, (h // 2) * LANES:(h // 2 + 1) * LANES]
            return jnp.where(lane_half == h % 2, block, zero)

        qs = [head_rows(h) for h in HEAD_ORDER]
        rows = slice(n * BLOCK, (n + 3) * BLOCK)
        vb = v[rows]
        s = jnp.concatenate(
            [_dot_nt(k[rows], jnp.concatenate(qs[:n_aligned], axis=0)),
             _dot_nt(k_swapped[rows], jnp.concatenate(qs[n_aligned:], axis=0))], axis=1)
        s = s + bias_ref[...]
        if n == 0:
            s = jnp.concatenate([s[0:BLOCK] + edge_first, s[BLOCK:]], axis=0)
        if n == nblk - 1:
            s = jnp.concatenate([s[:2 * BLOCK], s[2 * BLOCK:] + edge_last], axis=0)
        for work in side_work[n * len(side_work) // nblk:(n + 1) * len(side_work) // nblk]:
            work()
        m = jnp.maximum(jnp.max(s, axis=0, keepdims=True), sink2)
        p = jnp.exp2(s - m)
        denom = jnp.sum(p, axis=0, keepdims=True) + jnp.exp2(sink2 - m)
        o = _dot_tn(vb, p.astype(BF16)) * (1.0 / denom)
        for col, h in enumerate(HEAD_ORDER):
            kvh = h // Q_GROUP
            attn_t_ref[h * HEAD_DIM:(h + 1) * HEAD_DIM, n * BLOCK:(n + 1) * BLOCK] = (
                o[kvh * HEAD_DIM:(kvh + 1) * HEAD_DIM, col * BLOCK:(col + 1) * BLOCK].astype(BF16))


def _mix_kernel(seq, tm, layer, sink_ref, h_ref, hp_ref, hn_ref, gain_ref, win_ref, wg_ref, ps_ref, wpb_ref, wab_ref,
                wo_ref, out_ref, zext_ref, bias_ref, attn_t_ref, gate_ref, pooled_ref, merged_ref):
    @pl.when(pl.program_id(0) == 0)
    def _():
        _fill_bias(bias_ref)

    blocks_per_seq = seq // tm
    pos = pl.program_id(0) % blocks_per_seq
    gain = gain_ref[...]
    u = _rms_norm(h_ref[...], gain).astype(BF16)
    u_ext = jnp.concatenate(
        [_rms_norm(hp_ref[...], gain).astype(BF16), u, _rms_norm(hn_ref[...], gain).astype(BF16)], axis=0)

    z_lo = BLOCK - BF16_ROWS
    zp = _dot(u_ext[z_lo:z_lo + tm + 2 * BF16_ROWS], win_ref[:, 0:POOL_WIDTH])
    h8 = POOL_HALO
    skip = BF16_ROWS - h8
    zext_ref[0:h8, :] = jnp.where(pos > 0, zp[skip:skip + h8], 0.0)
    zext_ref[h8:h8 + tm, :] = zp[BF16_ROWS:BF16_ROWS + tm]
    zext_ref[h8 + tm:2 * h8 + tm, :] = jnp.where(
        pos < blocks_per_seq - 1, zp[BF16_ROWS + tm:BF16_ROWS + tm + h8], 0.0)
    zext_ref[2 * h8 + tm:, :] = jnp.zeros((2 * h8, POOL_WIDTH), F32)

    kv = _dot(u_ext, win_ref[:, KV_OFF:KV_OFF + 2 * KV_WIDTH])
    k = kv[:, 0:KV_WIDTH].astype(BF16)
    k_swapped = pltpu.roll(kv[:, 0:KV_WIDTH], HEAD_DIM, 1).astype(BF16)
    v = kv[:, KV_WIDTH:2 * KV_WIDTH].astype(BF16)
    q = (_dot(u, win_ref[:, Q_OFF:Q_OFF + Q_WIDTH]) * (LOG2E * HEAD_DIM ** -0.5)).astype(BF16)

    y = _pool_branch(pos, seq, tm, zext_ref, wg_ref, ps_ref).astype(BF16)

    def gate(c0):
        return 0.5 + 0.5 * jnp.tanh(0.5 * _dot(u, win_ref[:, GATE_OFF + c0:GATE_OFF + c0 + MXU_COLS]))

    def attn_gate_chunk(c0):
        gate_ref[:, c0:c0 + MXU_COLS] = gate(D_MODEL + c0)

    def pool_chunk(c0):
        cols = slice(c0, c0 + MXU_COLS)
        pooled_ref[:, cols] = gate(c0) * _dot(y, wpb_ref[:, cols])

    chunks = range(0, D_MODEL, MXU_COLS)
    side_work = ([functools.partial(attn_gate_chunk, c0) for c0 in chunks]
                 + [functools.partial(pool_chunk, c0) for c0 in chunks])
    _attention(pos, seq, tm, layer, sink_ref, q, k, k_swapped, v, bias_ref, attn_t_ref, side_work)
    attn_t = attn_t_ref[...]
    for c0 in chunks:
        cols = slice(c0, c0 + MXU_COLS)
        merged = pooled_ref[:, cols] + gate_ref[:, cols] * _dot_tn(attn_t, wab_ref[:, cols])
        merged_ref[:, cols] = merged.astype(BF16)
    out_ref[...] = h_ref[...] + _dot(merged_ref[...], wo_ref[...])


def _layer_spec(arr, layer):
    zeros = (0,) * (arr.ndim - 1)
    return pl.BlockSpec((None,) + arr.shape[1:], lambda i: (layer,) + zeros, pipeline_mode=pl.Buffered(1))


def _mix_call(h, layer, gain, w_in, sink, wg, ps, wpb, wab, wo, seq, tm):
    n = h.shape[0]
    row = lambda i: (i, 0)
    per = tm // BLOCK
    prev_blk = lambda i: (jnp.maximum(i * per - 1, 0), 0)
    next_blk = lambda i: (jnp.minimum((i + 1) * per, n // BLOCK - 1), 0)
    return pl.pallas_call(
        functools.partial(_mix_kernel, seq, tm, layer),
        grid=(n // tm,),
        in_specs=[
            pl.BlockSpec(memory_space=pltpu.SMEM),
            pl.BlockSpec((tm, D_MODEL), row),
            pl.BlockSpec((BLOCK, D_MODEL), prev_blk),
            pl.BlockSpec((BLOCK, D_MODEL), next_blk),
            _layer_spec(gain, layer),
            _layer_spec(w_in, layer),
            _layer_spec(wg, layer),
            _layer_spec(ps, layer),
            _layer_spec(wpb, layer),
            _layer_spec(wab, layer),
            _layer_spec(wo, layer),
        ],
        out_specs=pl.BlockSpec((tm, D_MODEL), row),
        out_shape=jax.ShapeDtypeStruct((n, D_MODEL), F32),
        scratch_shapes=[
            pltpu.VMEM((tm + 4 * POOL_HALO, POOL_WIDTH), F32),
            pltpu.VMEM((3 * BLOCK, N_Q_HEADS * BLOCK), F32),
            pltpu.VMEM((Q_WIDTH, tm), BF16),
            pltpu.VMEM((tm, D_MODEL), F32),
            pltpu.VMEM((tm, D_MODEL), F32),
            pltpu.VMEM((tm, D_MODEL), BF16),
        ],
        compiler_params=pltpu.CompilerParams(
            dimension_semantics=("arbitrary",), vmem_limit_bytes=VMEM_LIMIT_BYTES),
        name="mix",
    )(sink, h, h, h, gain, w_in, wg, ps, wpb, wab, wo)


def _stage_weights(pairs, layer, stage_ref, sem):
    n_win, win_rows, win_cols = stage_ref.shape
    chunks = []
    for src, dst in pairs:
        rows, cols = dst.shape
        assert rows % win_rows == 0 and cols <= win_cols
        chunks += [(src, dst, r0, cols) for r0 in range(0, rows, win_rows)]

    def copy(j):
        src, _, r0, cols = chunks[j]
        win = j % n_win
        return pltpu.make_async_copy(
            src.at[layer, pl.ds(r0, win_rows), :], stage_ref.at[win, :, pl.ds(0, cols)], sem.at[win])

    for j in range(min(n_win, len(chunks))):
        copy(j).start()
    for j, (_, dst, r0, cols) in enumerate(chunks):
        copy(j).wait()
        dst[r0:r0 + win_rows, :] = stage_ref[j % n_win, :, 0:cols].astype(BF16)
        if j + n_win < len(chunks):
            copy(j + n_win).start()


def _ffn_kernel(final, layer, h_ref, gain_ref, wg_hbm, wu_hbm, wd_hbm, fgain_ref, out_ref,
                wg_ref, wu_ref, wd_ref, stage_ref, sem):
    @pl.when(pl.program_id(0) == 0)
    def _():
        _stage_weights([(wg_hbm, wg_ref), (wu_hbm, wu_ref), (wd_hbm, wd_ref)], layer, stage_ref, sem)

    h = h_ref[...]
    u = _rms_norm(h, gain_ref[...]).astype(BF16)
    d_ff = wg_ref.shape[1]
    for c0 in range(0, d_ff, FFN_CHUNK):
        cols = slice(c0, min(c0 + FFN_CHUNK, d_ff))
        act = (jax.nn.silu(_dot(u, wg_ref[:, cols])) * _dot(u, wu_ref[:, cols])).astype(BF16)
        h = h + _dot(act, wd_ref[cols, :])
    out_ref[...] = _rms_norm(h, fgain_ref[...]) if final else h


def _ffn_call(h, layer, gain, wg, wu, wd, fgain, final, tm):
    n = h.shape[0]
    d_ff = wg.shape[2]
    row = lambda i: (i, 0)
    hbm = pl.BlockSpec(memory_space=pl.ANY)
    return pl.pallas_call(
        functools.partial(_ffn_kernel, final, layer),
        grid=(n // tm,),
        in_specs=[
            pl.BlockSpec((tm, D_MODEL), row),
            _layer_spec(gain, layer),
            hbm, hbm, hbm,
            pl.BlockSpec((1, D_MODEL), lambda i: (0, 0)),
        ],
        out_specs=pl.BlockSpec((tm, D_MODEL), row),
        out_shape=jax.ShapeDtypeStruct((n, D_MODEL), F32),
        scratch_shapes=[
            pltpu.VMEM((D_MODEL, d_ff), BF16),
            pltpu.VMEM((D_MODEL, d_ff), BF16),
            pltpu.VMEM((d_ff, D_MODEL), BF16),
            pltpu.VMEM((STAGE_WINDOWS, STAGE_ROWS, d_ff), F32),
            pltpu.SemaphoreType.DMA((STAGE_WINDOWS,)),
        ],
        compiler_params=pltpu.CompilerParams(
            dimension_semantics=("arbitrary",), vmem_limit_bytes=VMEM_LIMIT_BYTES),
        name="ffn",
    )(h, gain, wg, wu, wd, fgain)


def kernel(x, norm_mix, w_in, w_pool_group, pool_scale, sink, w_pool_branch, w_attn_branch, w_out,
           norm_ffn, w_ffn_gate, w_ffn_up, w_ffn_down, norm_final):
    batch, seq, _ = x.shape
    depth = w_in.shape[0]
    tm_mix, tm_ffn = 1024, 1024
    assert seq % tm_mix == 0 and tm_mix % BLOCK == 0 and (batch * seq) % tm_ffn == 0

    w_in_b, wg_b, wpb_b, wab_b, wo_b = (
        w.astype(BF16) for w in (w_in, w_pool_group, w_pool_branch, w_attn_branch, w_out))
    norm_mix3, pool_scale3, norm_ffn3 = (p[:, None, :] for p in (norm_mix, pool_scale, norm_ffn))

    h = x.reshape(batch * seq, D_MODEL)
    for l in range(depth):
        h = _mix_call(h, l, norm_mix3, w_in_b, sink, wg_b, pool_scale3, wpb_b, wab_b, wo_b, seq, tm_mix)
        h = _ffn_call(h, l, norm_ffn3, w_ffn_gate, w_ffn_up, w_ffn_down, norm_final[None, :], l == depth - 1,
                      tm_ffn)
    return h.reshape(batch, seq, D_MODEL)
```

```python
import functools

import jax
import jax.numpy as jnp
from jax import lax
from jax.experimental import pallas as pl
from jax.experimental.pallas import tpu as pltpu

D_MODEL = 1024
POOL_WIDTH = 512
POOL_WINDOWS = (2, 4, 8, 16)
POOL_GROUP = POOL_WIDTH // len(POOL_WINDOWS)
N_Q_HEADS = 8
N_KV_HEADS = 2
HEAD_DIM = 64
Q_GROUP = N_Q_HEADS // N_KV_HEADS
Q_WIDTH = N_Q_HEADS * HEAD_DIM
KV_WIDTH = N_KV_HEADS * HEAD_DIM
WINDOW = 128
BLOCK = 128
GATE_WIDTH = 2 * D_MODEL
IN_WIDTH = POOL_WIDTH + Q_WIDTH + 2 * KV_WIDTH + GATE_WIDTH
EPS = 1e-6
LOG2E = 1.4426950408889634

Q_OFF = POOL_WIDTH
KV_OFF = Q_OFF + Q_WIDTH
GATE_OFF = KV_OFF + 2 * KV_WIDTH

LANES = 128
MXU_COLS = 256
FFN_CHUNK = 4 * MXU_COLS
STAGE_ROWS = 128
STAGE_WINDOWS = 4
POOL_HALO = 8
BF16_ROWS = 16
assert LANES == 2 * HEAD_DIM and N_KV_HEADS == 2
ALIGNED_HEADS = tuple(h for h in range(N_Q_HEADS) if h % 2 == h // Q_GROUP)
HEAD_ORDER = ALIGNED_HEADS + tuple(h for h in range(N_Q_HEADS) if h not in ALIGNED_HEADS)
VMEM_LIMIT_BYTES = 56 * 1024 * 1024

F32 = jnp.float32
BF16 = jnp.bfloat16


def _rms_norm(x, gain):
    ms = jnp.mean(x * x, axis=-1, keepdims=True)
    return x * lax.rsqrt(ms + EPS) * gain


def _dot(a, b):
    return jnp.dot(a, b, preferred_element_type=F32)


def _dot_tn(a, b):
    return lax.dot_general(a, b, (((0,), (0,)), ((), ())), preferred_element_type=F32)


def _dot_nt(a, b):
    return lax.dot_general(a, b, (((1,), (1,)), ((), ())), preferred_element_type=F32)


def _pool_branch(pos, seq, tm, zext_ref, wg2_ref, scale):
    blocks_per_seq = seq // tm
    h8 = POOL_HALO
    r = lax.broadcasted_iota(jnp.int32, (h8, 1), 0)
    t_first = r
    t_last = seq - h8 + r
    ds = []
    for g, w in enumerate(POOL_WINDOWS):
        cols = slice(g * POOL_GROUP, (g + 1) * POOL_GROUP)
        half = w // 2
        if w == 2:
            s = zext_ref[h8 - 1:h8 - 1 + tm, cols] + zext_ref[h8:h8 + tm, cols]
        else:
            p2 = zext_ref[0:tm + 3 * h8, cols] + zext_ref[1:tm + 3 * h8 + 1, cols]
            if w == 4:
                s = p2[h8 - 2:h8 - 2 + tm] + p2[h8:h8 + tm]
            else:
                p4 = p2[0:tm + 2 * h8] + p2[2:tm + 2 * h8 + 2]
                if w == 8:
                    s = p4[h8 - 4:h8 - 4 + tm] + p4[h8:h8 + tm]
                else:
                    p8 = p4[0:tm + h8] + p4[4:tm + h8 + 4]
                    s = p8[0:tm] + p8[h8:h8 + tm]

        def inv_count(t, at_edge):
            count = (jnp.minimum(t + half, seq) - jnp.maximum(t - half, 0)).astype(F32)
            return jnp.where(at_edge, 1.0 / count, 1.0 / w)

        z = zext_ref[h8:h8 + tm, cols]
        d = jnp.concatenate([
            s[0:h8] * inv_count(t_first, pos == 0) - z[0:h8],
            s[h8:tm - h8] * (1.0 / w) - z[h8:tm - h8],
            s[tm - h8:] * inv_count(t_last, pos == blocks_per_seq - 1) - z[tm - h8:]], axis=0)
        ds.append(d.astype(BF16))
    ys = [_dot(jnp.concatenate(ds[2 * p:2 * p + 2], axis=1), wg2_ref[p]) for p in range(len(ds) // 2)]
    return jnp.concatenate(ys, axis=1) * scale


def _fill_group_pairs(wg_ref, wg2_ref):
    g = POOL_GROUP
    wg2_ref[...] = jnp.zeros(wg2_ref.shape, wg2_ref.dtype)
    for p in range(wg2_ref.shape[0]):
        wg2_ref[p, 0:g, 0:g] = wg_ref[2 * p]
        wg2_ref[p, g:2 * g, g:2 * g] = wg_ref[2 * p + 1]


def _fill_bias(bias_ref):
    c = lax.broadcasted_iota(jnp.int32, (3 * BLOCK, BLOCK), 0)
    a = lax.broadcasted_iota(jnp.int32, (3 * BLOCK, BLOCK), 1)
    absdist = jnp.abs(a - c + BLOCK)
    in_band = absdist <= WINDOW
    absdist_f = absdist.astype(F32)
    for col, h in enumerate(HEAD_ORDER):
        slope = 2.0 ** -(h + 1)
        bias_ref[:, col * BLOCK:(col + 1) * BLOCK] = jnp.where(in_band, (-slope * LOG2E) * absdist_f, -jnp.inf)


def _attention(pos, seq, tm, layer, sink_ref, q, k, k_swapped, v, bias_ref, attn_t_ref, side_work):
    nblk = tm // BLOCK
    lane_half = lax.broadcasted_iota(jnp.int32, (1, LANES), 1) // HEAD_DIM
    sink2 = jnp.concatenate(
        [jnp.full((1, BLOCK), sink_ref[layer, h] * LOG2E, F32) for h in HEAD_ORDER], axis=1)
    edge_first = jnp.where(pos > 0, 0.0, -jnp.inf)
    edge_last = jnp.where(pos < seq // tm - 1, 0.0, -jnp.inf)
    zero = jnp.zeros((), BF16)
    n_aligned = len(ALIGNED_HEADS)
    for n in range(nblk):
        qb = q[n * BLOCK:(n + 1) * BLOCK, :]

        def head_rows(h):
            block = qb[:, (h // 2) * LANES:(h // 2 + 1) * LANES]
            return jnp.where(lane_half == h % 2, block, zero)

        qs = [head_rows(h) for h in HEAD_ORDER]
        rows = slice(n * BLOCK, (n + 3) * BLOCK)
        vb = v[rows]
        s = jnp.concatenate(
            [_dot_nt(k[rows], jnp.concatenate(qs[:n_aligned], axis=0)),
             _dot_nt(k_swapped[rows], jnp.concatenate(qs[n_aligned:], axis=0))], axis=1)
        s = s + bias_ref[...]
        if n == 0:
            s = jnp.concatenate([s[0:BLOCK] + edge_first, s[BLOCK:]], axis=0)
        if n == nblk - 1:
            s = jnp.concatenate([s[:2 * BLOCK], s[2 * BLOCK:] + edge_last], axis=0)
        for work in side_work[n * len(side_work) // nblk:(n + 1) * len(side_work) // nblk]:
            work()
        m = jnp.maximum(jnp.max(s, axis=0, keepdims=True), sink2)
        p = jnp.exp2(s - m)
        denom = jnp.sum(p, axis=0, keepdims=True) + jnp.exp2(sink2 - m)
        o = _dot_tn(vb, p.astype(BF16)) * (1.0 / denom)
        for col, h in enumerate(HEAD_ORDER):
            kvh = h // Q_GROUP
            attn_t_ref[h * HEAD_DIM:(h + 1) * HEAD_DIM, n * BLOCK:(n + 1) * BLOCK] = (
                o[kvh * HEAD_DIM:(kvh + 1) * HEAD_DIM, col * BLOCK:(col + 1) * BLOCK].astype(BF16))


def _mix_kernel(seq, tm, layer, sink_ref, h_ref, hp_ref, hn_ref, gain_ref, win_ref, wg_ref, ps_ref, wpb_ref, wab_ref,
                wo_ref, out_ref, zext_ref, bias_ref, wg2_ref, attn_t_ref, gate_ref, pooled_ref, merged_ref):
    @pl.when(pl.program_id(0) == 0)
    def _():
        _fill_bias(bias_ref)
        _fill_group_pairs(wg_ref, wg2_ref)

    blocks_per_seq = seq // tm
    pos = pl.program_id(0) % blocks_per_seq
    gain = gain_ref[layer:layer + 1, :]

    def norm(x):
        return _rms_norm(x, gain).astype(BF16)

    half = tm // 2
    u_lo = jnp.concatenate([norm(hp_ref[...]), norm(h_ref[0:half, :])], axis=0)
    u_hi = jnp.concatenate([norm(h_ref[half:tm, :]), norm(hn_ref[...])], axis=0)
    u = jnp.concatenate([u_lo[BLOCK:], u_hi[:tm - half]], axis=0)

    def project(row_lo, row_hi, col0, width):
        w = win_ref[:, col0:col0 + width]
        return jnp.concatenate([_dot(u_lo[BLOCK + row_lo:], w), _dot(u_hi[:row_hi - half], w)], axis=0)

    zp = project(-BF16_ROWS, tm + BF16_ROWS, 0, POOL_WIDTH)
    h8 = POOL_HALO
    skip = BF16_ROWS - h8
    zext_ref[0:h8, :] = jnp.where(pos > 0, zp[skip:skip + h8], 0.0)
    zext_ref[h8:h8 + tm, :] = zp[BF16_ROWS:BF16_ROWS + tm]
    zext_ref[h8 + tm:2 * h8 + tm, :] = jnp.where(
        pos < blocks_per_seq - 1, zp[BF16_ROWS + tm:BF16_ROWS + tm + h8], 0.0)
    zext_ref[2 * h8 + tm:, :] = jnp.zeros((2 * h8, POOL_WIDTH), F32)

    kv = project(-BLOCK, tm + BLOCK, KV_OFF, 2 * KV_WIDTH)
    k = kv[:, 0:KV_WIDTH].astype(BF16)
    k_swapped = pltpu.roll(kv[:, 0:KV_WIDTH], HEAD_DIM, 1).astype(BF16)
    v = kv[:, KV_WIDTH:2 * KV_WIDTH].astype(BF16)
    q = (project(0, tm, Q_OFF, Q_WIDTH) * (LOG2E * HEAD_DIM ** -0.5)).astype(BF16)

    y = _pool_branch(pos, seq, tm, zext_ref, wg2_ref, ps_ref[layer:layer + 1, :]).astype(BF16)

    def gate(c0):
        return 0.5 + 0.5 * jnp.tanh(0.5 * _dot(u, win_ref[:, GATE_OFF + c0:GATE_OFF + c0 + MXU_COLS]))

    def attn_gate_chunk(c0):
        gate_ref[:, c0:c0 + MXU_COLS] = gate(D_MODEL + c0)

    def pool_chunk(c0):
        cols = slice(c0, c0 + MXU_COLS)
        pooled_ref[:, cols] = gate(c0) * _dot(y, wpb_ref[:, cols])

    chunks = range(0, D_MODEL, MXU_COLS)
    side_work = ([functools.partial(attn_gate_chunk, c0) for c0 in chunks]
                 + [functools.partial(pool_chunk, c0) for c0 in chunks])
    _attention(pos, seq, tm, layer, sink_ref, q, k, k_swapped, v, bias_ref, attn_t_ref, side_work)
    attn_t = attn_t_ref[...]
    for c0 in chunks:
        cols = slice(c0, c0 + MXU_COLS)
        merged = pooled_ref[:, cols] + gate_ref[:, cols] * _dot_tn(attn_t, wab_ref[:, cols])
        merged_ref[:, cols] = merged.astype(BF16)
    out_ref[...] = h_ref[...] + _dot(merged_ref[...], wo_ref[...])


def _layer_spec(arr, layer):
    zeros = (0,) * (arr.ndim - 1)
    return pl.BlockSpec((None,) + arr.shape[1:], lambda i: (layer,) + zeros, pipeline_mode=pl.Buffered(1))


def _whole_spec(arr):
    return pl.BlockSpec(arr.shape, lambda i: (0, 0))


def _mix_call(h, layer, gain, w_in, sink, wg, ps, wpb, wab, wo, seq, tm):
    n = h.shape[0]
    row = lambda i: (i, 0)
    per = tm // BLOCK
    prev_blk = lambda i: (jnp.maximum(i * per - 1, 0), 0)
    next_blk = lambda i: (jnp.minimum((i + 1) * per, n // BLOCK - 1), 0)
    return pl.pallas_call(
        functools.partial(_mix_kernel, seq, tm, layer),
        grid=(n // tm,),
        in_specs=[
            pl.BlockSpec(memory_space=pltpu.SMEM),
            pl.BlockSpec((tm, D_MODEL), row),
            pl.BlockSpec((BLOCK, D_MODEL), prev_blk),
            pl.BlockSpec((BLOCK, D_MODEL), next_blk),
            _whole_spec(gain),
            _layer_spec(w_in, layer),
            _layer_spec(wg, layer),
            _whole_spec(ps),
            _layer_spec(wpb, layer),
            _layer_spec(wab, layer),
            _layer_spec(wo, layer),
        ],
        out_specs=pl.BlockSpec((tm, D_MODEL), row),
        out_shape=jax.ShapeDtypeStruct((n, D_MODEL), F32),
        scratch_shapes=[
            pltpu.VMEM((tm + 4 * POOL_HALO, POOL_WIDTH), F32),
            pltpu.VMEM((3 * BLOCK, N_Q_HEADS * BLOCK), F32),
            pltpu.VMEM((len(POOL_WINDOWS) // 2, 2 * POOL_GROUP, 2 * POOL_GROUP), BF16),
            pltpu.VMEM((Q_WIDTH, tm), BF16),
            pltpu.VMEM((tm, D_MODEL), F32),
            pltpu.VMEM((tm, D_MODEL), F32),
            pltpu.VMEM((tm, D_MODEL), BF16),
        ],
        compiler_params=pltpu.CompilerParams(
            dimension_semantics=("arbitrary",), vmem_limit_bytes=VMEM_LIMIT_BYTES),
        name="mix",
    )(sink, h, h, h, gain, w_in, wg, ps, wpb, wab, wo)


def _stage_weights(pairs, layer, stage_ref, sem):
    n_win, win_rows, win_cols = stage_ref.shape
    chunks = []
    for src, dst in pairs:
        rows, cols = dst.shape
        assert rows % win_rows == 0 and cols <= win_cols
        chunks += [(src, dst, r0, cols) for r0 in range(0, rows, win_rows)]

    def copy(j):
        src, _, r0, cols = chunks[j]
        win = j % n_win
        return pltpu.make_async_copy(
            src.at[layer, pl.ds(r0, win_rows), :], stage_ref.at[win, :, pl.ds(0, cols)], sem.at[win])

    for j in range(min(n_win, len(chunks))):
        copy(j).start()
    for j, (_, dst, r0, cols) in enumerate(chunks):
        copy(j).wait()
        dst[r0:r0 + win_rows, :] = stage_ref[j % n_win, :, 0:cols].astype(BF16)
        if j + n_win < len(chunks):
            copy(j + n_win).start()


def _ffn_kernel(final, layer, h_ref, gain_ref, wg_hbm, wu_hbm, wd_hbm, fgain_ref, out_ref,
                wg_ref, wu_ref, wd_ref, stage_ref, sem):
    @pl.when(pl.program_id(0) == 0)
    def _():
        _stage_weights([(wg_hbm, wg_ref), (wu_hbm, wu_ref), (wd_hbm, wd_ref)], layer, stage_ref, sem)

    h = h_ref[...]
    u = _rms_norm(h, gain_ref[layer:layer + 1, :]).astype(BF16)
    d_ff = wg_ref.shape[1]
    for c0 in range(0, d_ff, FFN_CHUNK):
        cols = slice(c0, min(c0 + FFN_CHUNK, d_ff))
        act = (jax.nn.silu(_dot(u, wg_ref[:, cols])) * _dot(u, wu_ref[:, cols])).astype(BF16)
        h = h + _dot(act, wd_ref[cols, :])
    out_ref[...] = _rms_norm(h, fgain_ref[...]) if final else h


def _ffn_call(h, layer, gain, wg, wu, wd, fgain, final, tm):
    n = h.shape[0]
    d_ff = wg.shape[2]
    row = lambda i: (i, 0)
    hbm = pl.BlockSpec(memory_space=pl.ANY)
    return pl.pallas_call(
        functools.partial(_ffn_kernel, final, layer),
        grid=(n // tm,),
        in_specs=[
            pl.BlockSpec((tm, D_MODEL), row),
            _whole_spec(gain),
            hbm, hbm, hbm,
            pl.BlockSpec((1, D_MODEL), lambda i: (0, 0)),
        ],
        out_specs=pl.BlockSpec((tm, D_MODEL), row),
        out_shape=jax.ShapeDtypeStruct((n, D_MODEL), F32),
        scratch_shapes=[
            pltpu.VMEM((D_MODEL, d_ff), BF16),
            pltpu.VMEM((D_MODEL, d_ff), BF16),
            pltpu.VMEM((d_ff, D_MODEL), BF16),
            pltpu.VMEM((STAGE_WINDOWS, STAGE_ROWS, d_ff), F32),
            pltpu.SemaphoreType.DMA((STAGE_WINDOWS,)),
        ],
        compiler_params=pltpu.CompilerParams(
            dimension_semantics=("arbitrary",), vmem_limit_bytes=VMEM_LIMIT_BYTES),
        name="ffn",
    )(h, gain, wg, wu, wd, fgain)


def kernel(x, norm_mix, w_in, w_pool_group, pool_scale, sink, w_pool_branch, w_attn_branch, w_out,
           norm_ffn, w_ffn_gate, w_ffn_up, w_ffn_down, norm_final):
    batch, seq, _ = x.shape
    depth = w_in.shape[0]
    tm_mix, tm_ffn = 1024, 1024
    assert seq % tm_mix == 0 and tm_mix % BLOCK == 0 and (batch * seq) % tm_ffn == 0

    w_in_b, wg_b, wpb_b, wab_b, wo_b = (
        w.astype(BF16) for w in (w_in, w_pool_group, w_pool_branch, w_attn_branch, w_out))

    h = x.reshape(batch * seq, D_MODEL)
    for l in range(depth):
        h = _mix_call(h, l, norm_mix, w_in_b, sink, wg_b, pool_scale, wpb_b, wab_b, wo_b, seq, tm_mix)
        h = _ffn_call(h, l, norm_ffn, w_ffn_gate, w_ffn_up, w_ffn_down, norm_final[None, :], l == depth - 1,
                      tm_ffn)
    return h.reshape(batch, seq, D_MODEL)
```

```python
import functools

import jax
import jax.numpy as jnp
from jax import lax
from jax.experimental import pallas as pl
from jax.experimental.pallas import tpu as pltpu

D_MODEL = 1024
POOL_WIDTH = 512
POOL_WINDOWS = (2, 4, 8, 16)
POOL_GROUP = POOL_WIDTH // len(POOL_WINDOWS)
N_Q_HEADS = 8
N_KV_HEADS = 2
HEAD_DIM = 64
Q_GROUP = N_Q_HEADS // N_KV_HEADS
Q_WIDTH = N_Q_HEADS * HEAD_DIM
KV_WIDTH = N_KV_HEADS * HEAD_DIM
WINDOW = 128
BLOCK = 128
GATE_WIDTH = 2 * D_MODEL
IN_WIDTH = POOL_WIDTH + Q_WIDTH + 2 * KV_WIDTH + GATE_WIDTH
EPS = 1e-6
LOG2E = 1.4426950408889634

Q_OFF = POOL_WIDTH
KV_OFF = Q_OFF + Q_WIDTH
GATE_OFF = KV_OFF + 2 * KV_WIDTH

LANES = 128
MXU_COLS = 256
FFN_CHUNK = 4 * MXU_COLS
STAGE_ROWS = 128
STAGE_WINDOWS = 4
POOL_HALO = 8
BF16_ROWS = 16
assert LANES == 2 * HEAD_DIM and N_KV_HEADS == 2
ALIGNED_HEADS = tuple(h for h in range(N_Q_HEADS) if h % 2 == h // Q_GROUP)
HEAD_ORDER = ALIGNED_HEADS + tuple(h for h in range(N_Q_HEADS) if h not in ALIGNED_HEADS)
VMEM_LIMIT_BYTES = 56 * 1024 * 1024

F32 = jnp.float32
BF16 = jnp.bfloat16


def _rms_norm(x, gain):
    ms = jnp.mean(x * x, axis=-1, keepdims=True)
    return x * lax.rsqrt(ms + EPS) * gain


def _dot(a, b):
    return jnp.dot(a, b, preferred_element_type=F32)


def _dot_tn(a, b):
    return lax.dot_general(a, b, (((0,), (0,)), ((), ())), preferred_element_type=F32)


def _dot_nt(a, b):
    return lax.dot_general(a, b, (((1,), (1,)), ((), ())), preferred_element_type=F32)


def _pool_branch(pos, seq, tm, zext_ref, wg2_ref, scale):
    blocks_per_seq = seq // tm
    h8 = POOL_HALO
    r = lax.broadcasted_iota(jnp.int32, (h8, 1), 0)
    t_first = r
    t_last = seq - h8 + r
    ds = []
    for g, w in enumerate(POOL_WINDOWS):
        cols = slice(g * POOL_GROUP, (g + 1) * POOL_GROUP)
        half = w // 2
        if w == 2:
            s = zext_ref[h8 - 1:h8 - 1 + tm, cols] + zext_ref[h8:h8 + tm, cols]
        else:
            p2 = zext_ref[0:tm + 3 * h8, cols] + zext_ref[1:tm + 3 * h8 + 1, cols]
            if w == 4:
                s = p2[h8 - 2:h8 - 2 + tm] + p2[h8:h8 + tm]
            else:
                p4 = p2[0:tm + 2 * h8] + p2[2:tm + 2 * h8 + 2]
                if w == 8:
                    s = p4[h8 - 4:h8 - 4 + tm] + p4[h8:h8 + tm]
                else:
                    p8 = p4[0:tm + h8] + p4[4:tm + h8 + 4]
                    s = p8[0:tm] + p8[h8:h8 + tm]

        def inv_count(t, at_edge):
            count = (jnp.minimum(t + half, seq) - jnp.maximum(t - half, 0)).astype(F32)
            return jnp.where(at_edge, 1.0 / count, 1.0 / w)

        z = zext_ref[h8:h8 + tm, cols]
        d = jnp.concatenate([
            s[0:h8] * inv_count(t_first, pos == 0) - z[0:h8],
            s[h8:tm - h8] * (1.0 / w) - z[h8:tm - h8],
            s[tm - h8:] * inv_count(t_last, pos == blocks_per_seq - 1) - z[tm - h8:]], axis=0)
        ds.append(d.astype(BF16))
    ys = [_dot(jnp.concatenate(ds[2 * p:2 * p + 2], axis=1), wg2_ref[p]) for p in range(len(ds) // 2)]
    return jnp.concatenate(ys, axis=1) * scale


def _fill_group_pairs(wg_ref, wg2_ref):
    g = POOL_GROUP
    wg2_ref[...] = jnp.zeros(wg2_ref.shape, wg2_ref.dtype)
    for p in range(wg2_ref.shape[0]):
        wg2_ref[p, 0:g, 0:g] = wg_ref[2 * p]
        wg2_ref[p, g:2 * g, g:2 * g] = wg_ref[2 * p + 1]


def _fill_bias(bias_ref):
    c = lax.broadcasted_iota(jnp.int32, (3 * BLOCK, BLOCK), 0)
    a = lax.broadcasted_iota(jnp.int32, (3 * BLOCK, BLOCK), 1)
    absdist = jnp.abs(a - c + BLOCK)
    in_band = absdist <= WINDOW
    absdist_f = absdist.astype(F32)
    for col, h in enumerate(HEAD_ORDER):
        slope = 2.0 ** -(h + 1)
        bias_ref[:, col * BLOCK:(col + 1) * BLOCK] = jnp.where(in_band, (-slope * LOG2E) * absdist_f, -jnp.inf)


def _attention(pos, seq, tm, layer, sink_ref, q, k, k_swapped, v, bias_ref, attn_t_ref, side_work):
    nblk = tm // BLOCK
    lane_half = lax.broadcasted_iota(jnp.int32, (1, LANES), 1) // HEAD_DIM
    sink2 = jnp.concatenate(
        [jnp.full((1, BLOCK), sink_ref[layer, h] * LOG2E, F32) for h in HEAD_ORDER], axis=1)
    edge_first = jnp.where(pos > 0, 0.0, -jnp.inf)
    edge_last = jnp.where(pos < seq // tm - 1, 0.0, -jnp.inf)
    zero = jnp.zeros((), BF16)
    n_aligned = len(ALIGNED_HEADS)
    for n in range(nblk):
        qb = q[n * BLOCK:(n + 1) * BLOCK, :]

        def head_rows(h):
            block = qb[:, (h // 2) * LANES:(h // 2 + 1) * LANES]
            return jnp.where(lane_half == h % 2, block, zero)

        qs = [head_rows(h) for h in HEAD_ORDER]
        rows = slice(n * BLOCK, (n + 3) * BLOCK)
        vb = v[rows]
        s = jnp.concatenate(
            [_dot_nt(k[rows], jnp.concatenate(qs[:n_aligned], axis=0)),
             _dot_nt(k_swapped[rows], jnp.concatenate(qs[n_aligned:], axis=0))], axis=1)
        s = s + bias_ref[...]
        if n == 0:
            s = jnp.concatenate([s[0:BLOCK] + edge_first, s[BLOCK:]], axis=0)
        if n == nblk - 1:
            s = jnp.concatenate([s[:2 * BLOCK], s[2 * BLOCK:] + edge_last], axis=0)
        for work in side_work[n * len(side_work) // nblk:(n + 1) * len(side_work) // nblk]:
            work()
        m = jnp.maximum(jnp.max(s, axis=0, keepdims=True), sink2)
        p = jnp.exp2(s - m)
        denom = jnp.sum(p, axis=0, keepdims=True) + jnp.exp2(sink2 - m)
        p = p.astype(BF16)
        inv = 1.0 / denom
        for kvh in range(N_KV_HEADS):
            cols = [c for c, h in enumerate(HEAD_ORDER) if h // Q_GROUP == kvh]
            p_kv = jnp.concatenate([p[:, c * BLOCK:(c + 1) * BLOCK] for c in cols], axis=1)
            inv_kv = jnp.concatenate([inv[:, c * BLOCK:(c + 1) * BLOCK] for c in cols], axis=1)
            o = _dot_tn(vb[:, kvh * HEAD_DIM:(kvh + 1) * HEAD_DIM], p_kv) * inv_kv
            for j, c in enumerate(cols):
                h = HEAD_ORDER[c]
                attn_t_ref[h * HEAD_DIM:(h + 1) * HEAD_DIM, n * BLOCK:(n + 1) * BLOCK] = (
                    o[:, j * BLOCK:(j + 1) * BLOCK].astype(BF16))


def _mix_kernel(seq, tm, layer, sink_ref, h_ref, hp_ref, hn_ref, gain_ref, win_ref, wg_ref, ps_ref, wpb_ref, wab_ref,
                wo_ref, out_ref, zext_ref, bias_ref, wg2_ref, attn_t_ref, gate_ref, pooled_ref, merged_ref):
    @pl.when(pl.program_id(0) == 0)
    def _():
        _fill_bias(bias_ref)
        _fill_group_pairs(wg_ref, wg2_ref)

    blocks_per_seq = seq // tm
    pos = pl.program_id(0) % blocks_per_seq
    gain = gain_ref[layer:layer + 1, :]

    def norm(x):
        return _rms_norm(x, gain).astype(BF16)

    half = tm // 2
    u_lo = jnp.concatenate([norm(hp_ref[...]), norm(h_ref[0:half, :])], axis=0)
    u_hi = jnp.concatenate([norm(h_ref[half:tm, :]), norm(hn_ref[...])], axis=0)
    u = jnp.concatenate([u_lo[BLOCK:], u_hi[:tm - half]], axis=0)

    def project(row_lo, row_hi, col0, width):
        w = win_ref[:, col0:col0 + width]
        return jnp.concatenate([_dot(u_lo[BLOCK + row_lo:], w), _dot(u_hi[:row_hi - half], w)], axis=0)

    zp = project(-BF16_ROWS, tm + BF16_ROWS, 0, POOL_WIDTH)
    h8 = POOL_HALO
    skip = BF16_ROWS - h8
    zext_ref[0:h8, :] = jnp.where(pos > 0, zp[skip:skip + h8], 0.0)
    zext_ref[h8:h8 + tm, :] = zp[BF16_ROWS:BF16_ROWS + tm]
    zext_ref[h8 + tm:2 * h8 + tm, :] = jnp.where(
        pos < blocks_per_seq - 1, zp[BF16_ROWS + tm:BF16_ROWS + tm + h8], 0.0)
    zext_ref[2 * h8 + tm:, :] = jnp.zeros((2 * h8, POOL_WIDTH), F32)

    kv = project(-BLOCK, tm + BLOCK, KV_OFF, 2 * KV_WIDTH)
    k = kv[:, 0:KV_WIDTH].astype(BF16)
    k_swapped = pltpu.roll(kv[:, 0:KV_WIDTH], HEAD_DIM, 1).astype(BF16)
    v = kv[:, KV_WIDTH:2 * KV_WIDTH].astype(BF16)
    q = (project(0, tm, Q_OFF, Q_WIDTH) * (LOG2E * HEAD_DIM ** -0.5)).astype(BF16)

    y = _pool_branch(pos, seq, tm, zext_ref, wg2_ref, ps_ref[layer:layer + 1, :]).astype(BF16)

    def gate(c0):
        return 0.5 + 0.5 * jnp.tanh(0.5 * _dot(u, win_ref[:, GATE_OFF + c0:GATE_OFF + c0 + MXU_COLS]))

    def attn_gate_chunk(c0):
        gate_ref[:, c0:c0 + MXU_COLS] = gate(D_MODEL + c0)

    def pool_chunk(c0):
        cols = slice(c0, c0 + MXU_COLS)
        pooled_ref[:, cols] = gate(c0) * _dot(y, wpb_ref[:, cols])

    chunks = range(0, D_MODEL, MXU_COLS)
    side_work = ([functools.partial(attn_gate_chunk, c0) for c0 in chunks]
                 + [functools.partial(pool_chunk, c0) for c0 in chunks])
    _attention(pos, seq, tm, layer, sink_ref, q, k, k_swapped, v, bias_ref, attn_t_ref, side_work)
    attn_t = attn_t_ref[...]
    for c0 in chunks:
        cols = slice(c0, c0 + MXU_COLS)
        merged = pooled_ref[:, cols] + gate_ref[:, cols] * _dot_tn(attn_t, wab_ref[:, cols])
        merged_ref[:, cols] = merged.astype(BF16)
    out_ref[...] = h_ref[...] + _dot(merged_ref[...], wo_ref[...])


def _layer_spec(arr, layer):
    zeros = (0,) * (arr.ndim - 1)
    return pl.BlockSpec((None,) + arr.shape[1:], lambda i: (layer,) + zeros, pipeline_mode=pl.Buffered(1))


def _whole_spec(arr):
    return pl.BlockSpec(arr.shape, lambda i: (0, 0))


def _mix_call(h, layer, gain, w_in, sink, wg, ps, wpb, wab, wo, seq, tm):
    n = h.shape[0]
    row = lambda i: (i, 0)
    per = tm // BLOCK
    prev_blk = lambda i: (jnp.maximum(i * per - 1, 0), 0)
    next_blk = lambda i: (jnp.minimum((i + 1) * per, n // BLOCK - 1), 0)
    return pl.pallas_call(
        functools.partial(_mix_kernel, seq, tm, layer),
        grid=(n // tm,),
        in_specs=[
            pl.BlockSpec(memory_space=pltpu.SMEM),
            pl.BlockSpec((tm, D_MODEL), row),
            pl.BlockSpec((BLOCK, D_MODEL), prev_blk),
            pl.BlockSpec((BLOCK, D_MODEL), next_blk),
            _whole_spec(gain),
            _layer_spec(w_in, layer),
            _layer_spec(wg, layer),
            _whole_spec(ps),
            _layer_spec(wpb, layer),
            _layer_spec(wab, layer),
            _layer_spec(wo, layer),
        ],
        out_specs=pl.BlockSpec((tm, D_MODEL), row),
        out_shape=jax.ShapeDtypeStruct((n, D_MODEL), F32),
        scratch_shapes=[
            pltpu.VMEM((tm + 4 * POOL_HALO, POOL_WIDTH), F32),
            pltpu.VMEM((3 * BLOCK, N_Q_HEADS * BLOCK), F32),
            pltpu.VMEM((len(POOL_WINDOWS) // 2, 2 * POOL_GROUP, 2 * POOL_GROUP), BF16),
            pltpu.VMEM((Q_WIDTH, tm), BF16),
            pltpu.VMEM((tm, D_MODEL), F32),
            pltpu.VMEM((tm, D_MODEL), F32),
            pltpu.VMEM((tm, D_MODEL), BF16),
        ],
        compiler_params=pltpu.CompilerParams(
            dimension_semantics=("arbitrary",), vmem_limit_bytes=VMEM_LIMIT_BYTES),
        name="mix",
    )(sink, h, h, h, gain, w_in, wg, ps, wpb, wab, wo)


def _stage_weights(pairs, layer, stage_ref, sem):
    n_win, win_rows, win_cols = stage_ref.shape
    chunks = []
    for src, dst in pairs:
        rows, cols = dst.shape
        assert rows % win_rows == 0 and cols <= win_cols
        chunks += [(src, dst, r0, cols) for r0 in range(0, rows, win_rows)]

    def copy(j):
        src, _, r0, cols = chunks[j]
        win = j % n_win
        return pltpu.make_async_copy(
            src.at[layer, pl.ds(r0, win_rows), :], stage_ref.at[win, :, pl.ds(0, cols)], sem.at[win])

    for j in range(min(n_win, len(chunks))):
        copy(j).start()
    for j, (_, dst, r0, cols) in enumerate(chunks):
        copy(j).wait()
        dst[r0:r0 + win_rows, :] = stage_ref[j % n_win, :, 0:cols].astype(BF16)
        if j + n_win < len(chunks):
            copy(j + n_win).start()


def _ffn_kernel(final, layer, h_ref, gain_ref, wg_hbm, wu_hbm, wd_hbm, fgain_ref, out_ref,
                wg_ref, wu_ref, wd_ref, stage_ref, sem):
    @pl.when(pl.program_id(0) == 0)
    def _():
        _stage_weights([(wg_hbm, wg_ref), (wu_hbm, wu_ref), (wd_hbm, wd_ref)], layer, stage_ref, sem)

    h = h_ref[...]
    u = _rms_norm(h, gain_ref[layer:layer + 1, :]).astype(BF16)
    d_ff = wg_ref.shape[1]
    for c0 in range(0, d_ff, FFN_CHUNK):
        cols = slice(c0, min(c0 + FFN_CHUNK, d_ff))
        act = (jax.nn.silu(_dot(u, wg_ref[:, cols])) * _dot(u, wu_ref[:, cols])).astype(BF16)
        h = h + _dot(act, wd_ref[cols, :])
    out_ref[...] = _rms_norm(h, fgain_ref[...]) if final else h


def _ffn_call(h, layer, gain, wg, wu, wd, fgain, final, tm):
    n = h.shape[0]
    d_ff = wg.shape[2]
    row = lambda i: (i, 0)
    hbm = pl.BlockSpec(memory_space=pl.ANY)
    return pl.pallas_call(
        functools.partial(_ffn_kernel, final, layer),
        grid=(n // tm,),
        in_specs=[
            pl.BlockSpec((tm, D_MODEL), row),
            _whole_spec(gain),
            hbm, hbm, hbm,
            pl.BlockSpec((1, D_MODEL), lambda i: (0, 0)),
        ],
        out_specs=pl.BlockSpec((tm, D_MODEL), row),
        out_shape=jax.ShapeDtypeStruct((n, D_MODEL), F32),
        scratch_shapes=[
            pltpu.VMEM((D_MODEL, d_ff), BF16),
            pltpu.VMEM((D_MODEL, d_ff), BF16),
            pltpu.VMEM((d_ff, D_MODEL), BF16),
            pltpu.VMEM((STAGE_WINDOWS, STAGE_ROWS, d_ff), F32),
            pltpu.SemaphoreType.DMA((STAGE_WINDOWS,)),
        ],
        compiler_params=pltpu.CompilerParams(
            dimension_semantics=("arbitrary",), vmem_limit_bytes=VMEM_LIMIT_BYTES),
        name="ffn",
    )(h, gain, wg, wu, wd, fgain)


def kernel(x, norm_mix, w_in, w_pool_group, pool_scale, sink, w_pool_branch, w_attn_branch, w_out,
           norm_ffn, w_ffn_gate, w_ffn_up, w_ffn_down, norm_final):
    batch, seq, _ = x.shape
    depth = w_in.shape[0]
    tm_mix, tm_ffn = 1024, 1024
    assert seq % tm_mix == 0 and tm_mix % BLOCK == 0 and (batch * seq) % tm_ffn == 0

    w_in_b, wg_b, wpb_b, wab_b, wo_b = (
        w.astype(BF16) for w in (w_in, w_pool_group, w_pool_branch, w_attn_branch, w_out))

    h = x.reshape(batch * seq, D_MODEL)
    for l in range(depth):
        h = _mix_call(h, l, norm_mix, w_in_b, sink, wg_b, pool_scale, wpb_b, wab_b, wo_b, seq, tm_mix)
        h = _ffn_call(h, l, norm_ffn, w_ffn_gate, w_ffn_up, w_ffn_down, norm_final[None, :], l == depth - 1,
                      tm_ffn)
    return h.reshape(batch, seq, D_MODEL)
```

```python
import functools

import jax
import jax.numpy as jnp
from jax import lax
from jax.experimental import pallas as pl
from jax.experimental.pallas import tpu as pltpu

D_MODEL = 1024
POOL_WIDTH = 512
POOL_WINDOWS = (2, 4, 8, 16)
POOL_GROUP = POOL_WIDTH // len(POOL_WINDOWS)
N_Q_HEADS = 8
N_KV_HEADS = 2
HEAD_DIM = 64
Q_GROUP = N_Q_HEADS // N_KV_HEADS
Q_WIDTH = N_Q_HEADS * HEAD_DIM
KV_WIDTH = N_KV_HEADS * HEAD_DIM
WINDOW = 128
BLOCK = 128
GATE_WIDTH = 2 * D_MODEL
IN_WIDTH = POOL_WIDTH + Q_WIDTH + 2 * KV_WIDTH + GATE_WIDTH
EPS = 1e-6
LOG2E = 1.4426950408889634

Q_OFF = POOL_WIDTH
KV_OFF = Q_OFF + Q_WIDTH
GATE_OFF = KV_OFF + 2 * KV_WIDTH

LANES = 128
MXU_COLS = 256
FFN_CHUNK = 4 * MXU_COLS
POOL_HALO = 8
BF16_ROWS = 16
assert LANES == 2 * HEAD_DIM and N_KV_HEADS == 2
ALIGNED_HEADS = tuple(h for h in range(N_Q_HEADS) if h % 2 == h // Q_GROUP)
HEAD_ORDER = ALIGNED_HEADS + tuple(h for h in range(N_Q_HEADS) if h not in ALIGNED_HEADS)
VMEM_LIMIT_BYTES = 56 * 1024 * 1024

F32 = jnp.float32
BF16 = jnp.bfloat16


def _rms_norm(x, gain):
    ms = jnp.mean(x * x, axis=-1, keepdims=True)
    return x * lax.rsqrt(ms + EPS) * gain


def _dot(a, b):
    return jnp.dot(a, b, preferred_element_type=F32)


def _dot_tn(a, b):
    return lax.dot_general(a, b, (((0,), (0,)), ((), ())), preferred_element_type=F32)


def _dot_nt(a, b):
    return lax.dot_general(a, b, (((1,), (1,)), ((), ())), preferred_element_type=F32)


def _pool_branch(pos, seq, tm, zext_ref, wg2_ref, scale):
    blocks_per_seq = seq // tm
    h8 = POOL_HALO
    r = lax.broadcasted_iota(jnp.int32, (h8, 1), 0)
    t_first = r
    t_last = seq - h8 + r
    ds = []
    for g, w in enumerate(POOL_WINDOWS):
        cols = slice(g * POOL_GROUP, (g + 1) * POOL_GROUP)
        half = w // 2
        if w == 2:
            s = zext_ref[h8 - 1:h8 - 1 + tm, cols] + zext_ref[h8:h8 + tm, cols]
        else:
            p2 = zext_ref[0:tm + 3 * h8, cols] + zext_ref[1:tm + 3 * h8 + 1, cols]
            if w == 4:
                s = p2[h8 - 2:h8 - 2 + tm] + p2[h8:h8 + tm]
            else:
                p4 = p2[0:tm + 2 * h8] + p2[2:tm + 2 * h8 + 2]
                if w == 8:
                    s = p4[h8 - 4:h8 - 4 + tm] + p4[h8:h8 + tm]
                else:
                    p8 = p4[0:tm + h8] + p4[4:tm + h8 + 4]
                    s = p8[0:tm] + p8[h8:h8 + tm]

        def inv_count(t, at_edge):
            count = (jnp.minimum(t + half, seq) - jnp.maximum(t - half, 0)).astype(F32)
            return jnp.where(at_edge, 1.0 / count, 1.0 / w)

        z = zext_ref[h8:h8 + tm, cols]
        d = jnp.concatenate([
            s[0:h8] * inv_count(t_first, pos == 0) - z[0:h8],
            s[h8:tm - h8] * (1.0 / w) - z[h8:tm - h8],
            s[tm - h8:] * inv_count(t_last, pos == blocks_per_seq - 1) - z[tm - h8:]], axis=0)
        ds.append(d.astype(BF16))
    ys = [_dot(jnp.concatenate(ds[2 * p:2 * p + 2], axis=1), wg2_ref[p]) for p in range(len(ds) // 2)]
    return jnp.concatenate(ys, axis=1) * scale


def _fill_group_pairs(wg_ref, wg2_ref):
    g = POOL_GROUP
    wg2_ref[...] = jnp.zeros(wg2_ref.shape, wg2_ref.dtype)
    for p in range(wg2_ref.shape[0]):
        wg2_ref[p, 0:g, 0:g] = wg_ref[2 * p]
        wg2_ref[p, g:2 * g, g:2 * g] = wg_ref[2 * p + 1]


def _fill_bias(bias_ref):
    c = lax.broadcasted_iota(jnp.int32, (3 * BLOCK, BLOCK), 0)
    a = lax.broadcasted_iota(jnp.int32, (3 * BLOCK, BLOCK), 1)
    absdist = jnp.abs(a - c + BLOCK)
    in_band = absdist <= WINDOW
    absdist_f = absdist.astype(F32)
    for col, h in enumerate(HEAD_ORDER):
        slope = 2.0 ** -(h + 1)
        bias_ref[:, col * BLOCK:(col + 1) * BLOCK] = jnp.where(in_band, (-slope * LOG2E) * absdist_f, -jnp.inf)


def _attention(pos, seq, tm, layer, sink_ref, q, k, k_swapped, v, bias_ref, attn_t_ref, side_work):
    nblk = tm // BLOCK
    lane_half = lax.broadcasted_iota(jnp.int32, (1, LANES), 1) // HEAD_DIM
    sink2 = jnp.concatenate(
        [jnp.full((1, BLOCK), sink_ref[layer, h] * LOG2E, F32) for h in HEAD_ORDER], axis=1)
    edge_first = jnp.where(pos > 0, 0.0, -jnp.inf)
    edge_last = jnp.where(pos < seq // tm - 1, 0.0, -jnp.inf)
    zero = jnp.zeros((), BF16)
    n_aligned = len(ALIGNED_HEADS)
    for n in range(nblk):
        qb = q[n * BLOCK:(n + 1) * BLOCK, :]

        def head_rows(h):
            block = qb[:, (h // 2) * LANES:(h // 2 + 1) * LANES]
            return jnp.where(lane_half == h % 2, block, zero)

        qs = [head_rows(h) for h in HEAD_ORDER]
        rows = slice(n * BLOCK, (n + 3) * BLOCK)
        vb = v[rows]
        s = jnp.concatenate(
            [_dot_nt(k[rows], jnp.concatenate(qs[:n_aligned], axis=0)),
             _dot_nt(k_swapped[rows], jnp.concatenate(qs[n_aligned:], axis=0))], axis=1)
        s = s + bias_ref[...]
        if n == 0:
            s = jnp.concatenate([s[0:BLOCK] + edge_first, s[BLOCK:]], axis=0)
        if n == nblk - 1:
            s = jnp.concatenate([s[:2 * BLOCK], s[2 * BLOCK:] + edge_last], axis=0)
        for work in side_work[n * len(side_work) // nblk:(n + 1) * len(side_work) // nblk]:
            work()
        m = jnp.maximum(jnp.max(s, axis=0, keepdims=True), sink2)
        p = jnp.exp2(s - m)
        denom = jnp.sum(p, axis=0, keepdims=True) + jnp.exp2(sink2 - m)
        p = p.astype(BF16)
        inv = 1.0 / denom
        for kvh in range(N_KV_HEADS):
            cols = [c for c, h in enumerate(HEAD_ORDER) if h // Q_GROUP == kvh]
            p_kv = jnp.concatenate([p[:, c * BLOCK:(c + 1) * BLOCK] for c in cols], axis=1)
            inv_kv = jnp.concatenate([inv[:, c * BLOCK:(c + 1) * BLOCK] for c in cols], axis=1)
            o = _dot_tn(vb[:, kvh * HEAD_DIM:(kvh + 1) * HEAD_DIM], p_kv) * inv_kv
            for j, c in enumerate(cols):
                h = HEAD_ORDER[c]
                attn_t_ref[h * HEAD_DIM:(h + 1) * HEAD_DIM, n * BLOCK:(n + 1) * BLOCK] = (
                    o[:, j * BLOCK:(j + 1) * BLOCK].astype(BF16))


def _convert_slabs(src_refs, dst_refs):
    for src, dst in zip(src_refs, dst_refs):
        dst[...] = src[...].astype(BF16)


def _mix_kernel(seq, tm, layer, n_convert, sink_ref, h_ref, hp_ref, hn_ref, gain_ref, win_ref, wg_ref, ps_ref,
                wpb_ref, wab_ref, wo_ref, *refs):
    convert_src, (out_ref, *convert_dst), scratch = refs[:n_convert], refs[n_convert:2 * n_convert + 1], refs[
        2 * n_convert + 1:]
    zext_ref, bias_ref, wg2_ref, attn_t_ref, gate_ref, pooled_ref, merged_ref = scratch
    _convert_slabs(convert_src, convert_dst)

    @pl.when(pl.program_id(0) == 0)
    def _():
        _fill_bias(bias_ref)
        _fill_group_pairs(wg_ref, wg2_ref)

    blocks_per_seq = seq // tm
    pos = pl.program_id(0) % blocks_per_seq
    gain = gain_ref[layer:layer + 1, :]

    def norm(x):
        return _rms_norm(x, gain).astype(BF16)

    half = tm // 2
    u_lo = jnp.concatenate([norm(hp_ref[...]), norm(h_ref[0:half, :])], axis=0)
    u_hi = jnp.concatenate([norm(h_ref[half:tm, :]), norm(hn_ref[...])], axis=0)
    u = jnp.concatenate([u_lo[BLOCK:], u_hi[:tm - half]], axis=0)

    def project(row_lo, row_hi, col0, width):
        w = win_ref[:, col0:col0 + width]
        return jnp.concatenate([_dot(u_lo[BLOCK + row_lo:], w), _dot(u_hi[:row_hi - half], w)], axis=0)

    zp = project(-BF16_ROWS, tm + BF16_ROWS, 0, POOL_WIDTH)
    h8 = POOL_HALO
    skip = BF16_ROWS - h8
    zext_ref[0:h8, :] = jnp.where(pos > 0, zp[skip:skip + h8], 0.0)
    zext_ref[h8:h8 + tm, :] = zp[BF16_ROWS:BF16_ROWS + tm]
    zext_ref[h8 + tm:2 * h8 + tm, :] = jnp.where(
        pos < blocks_per_seq - 1, zp[BF16_ROWS + tm:BF16_ROWS + tm + h8], 0.0)
    zext_ref[2 * h8 + tm:, :] = jnp.zeros((2 * h8, POOL_WIDTH), F32)

    kv = project(-BLOCK, tm + BLOCK, KV_OFF, 2 * KV_WIDTH)
    k = kv[:, 0:KV_WIDTH].astype(BF16)
    k_swapped = pltpu.roll(kv[:, 0:KV_WIDTH], HEAD_DIM, 1).astype(BF16)
    v = kv[:, KV_WIDTH:2 * KV_WIDTH].astype(BF16)
    q = (project(0, tm, Q_OFF, Q_WIDTH) * (LOG2E * HEAD_DIM ** -0.5)).astype(BF16)

    y = _pool_branch(pos, seq, tm, zext_ref, wg2_ref, ps_ref[layer:layer + 1, :]).astype(BF16)

    def gate(c0):
        return 0.5 + 0.5 * jnp.tanh(0.5 * _dot(u, win_ref[:, GATE_OFF + c0:GATE_OFF + c0 + MXU_COLS]))

    def attn_gate_chunk(c0):
        gate_ref[:, c0:c0 + MXU_COLS] = gate(D_MODEL + c0)

    def pool_chunk(c0):
        cols = slice(c0, c0 + MXU_COLS)
        pooled_ref[:, cols] = gate(c0) * _dot(y, wpb_ref[:, cols])

    chunks = range(0, D_MODEL, MXU_COLS)
    side_work = ([functools.partial(attn_gate_chunk, c0) for c0 in chunks]
                 + [functools.partial(pool_chunk, c0) for c0 in chunks])
    _attention(pos, seq, tm, layer, sink_ref, q, k, k_swapped, v, bias_ref, attn_t_ref, side_work)
    attn_t = attn_t_ref[...]
    for c0 in chunks:
        cols = slice(c0, c0 + MXU_COLS)
        merged = pooled_ref[:, cols] + gate_ref[:, cols] * _dot_tn(attn_t, wab_ref[:, cols])
        merged_ref[:, cols] = merged.astype(BF16)
    out_ref[...] = h_ref[...] + _dot(merged_ref[...], wo_ref[...])


def _resident_spec(arr):
    zeros = (0,) * arr.ndim
    return pl.BlockSpec(arr.shape, lambda i: zeros, pipeline_mode=pl.Buffered(1))


def _whole_spec(arr):
    return pl.BlockSpec(arr.shape, lambda i: (0, 0))


def _convert_specs(weights, layer, n_steps):
    in_specs, out_specs, out_shapes = [], [], []
    for w in weights:
        _, rows, cols = w.shape
        slab = rows // n_steps
        assert rows % n_steps == 0 and slab % BF16_ROWS == 0
        in_specs.append(pl.BlockSpec((None, slab, cols), lambda i: (layer, i, 0)))
        out_specs.append(pl.BlockSpec((slab, cols), lambda i: (i, 0)))
        out_shapes.append(jax.ShapeDtypeStruct((rows, cols), BF16))
    return in_specs, out_specs, out_shapes


def _mix_call(h, layer, gain, w_in, sink, wg, ps, wpb, wab, wo, convert, seq, tm):
    n = h.shape[0]
    row = lambda i: (i, 0)
    per = tm // BLOCK
    prev_blk = lambda i: (jnp.maximum(i * per - 1, 0), 0)
    next_blk = lambda i: (jnp.minimum((i + 1) * per, n // BLOCK - 1), 0)
    cv_in, cv_out, cv_shapes = _convert_specs(convert, layer, n // tm)
    out, *converted = pl.pallas_call(
        functools.partial(_mix_kernel, seq, tm, layer, len(convert)),
        grid=(n // tm,),
        in_specs=[
            pl.BlockSpec(memory_space=pltpu.SMEM),
            pl.BlockSpec((tm, D_MODEL), row),
            pl.BlockSpec((BLOCK, D_MODEL), prev_blk),
            pl.BlockSpec((BLOCK, D_MODEL), next_blk),
            _whole_spec(gain),
            _resident_spec(w_in),
            _resident_spec(wg),
            _whole_spec(ps),
            _resident_spec(wpb),
            _resident_spec(wab),
            _resident_spec(wo),
        ] + cv_in,
        out_specs=[pl.BlockSpec((tm, D_MODEL), row)] + cv_out,
        out_shape=[jax.ShapeDtypeStruct((n, D_MODEL), F32)] + cv_shapes,
        scratch_shapes=[
            pltpu.VMEM((tm + 4 * POOL_HALO, POOL_WIDTH), F32),
            pltpu.VMEM((3 * BLOCK, N_Q_HEADS * BLOCK), F32),
            pltpu.VMEM((len(POOL_WINDOWS) // 2, 2 * POOL_GROUP, 2 * POOL_GROUP), BF16),
            pltpu.VMEM((Q_WIDTH, tm), BF16),
            pltpu.VMEM((tm, D_MODEL), F32),
            pltpu.VMEM((tm, D_MODEL), F32),
            pltpu.VMEM((tm, D_MODEL), BF16),
        ],
        compiler_params=pltpu.CompilerParams(
            dimension_semantics=("arbitrary",), vmem_limit_bytes=VMEM_LIMIT_BYTES),
        name="mix",
    )(sink, h, h, h, gain, w_in, wg, ps, wpb, wab, wo, *convert)
    return out, converted


def _ffn_kernel(final, layer, n_convert, h_ref, gain_ref, wg_ref, wu_ref, wd_ref, fgain_ref, *refs):
    convert_src, (out_ref, *convert_dst) = refs[:n_convert], refs[n_convert:]
    _convert_slabs(convert_src, convert_dst)
    h = h_ref[...]
    u = _rms_norm(h, gain_ref[layer:layer + 1, :]).astype(BF16)
    d_ff = wg_ref.shape[1]
    for c0 in range(0, d_ff, FFN_CHUNK):
        cols = slice(c0, min(c0 + FFN_CHUNK, d_ff))
        act = (jax.nn.silu(_dot(u, wg_ref[:, cols])) * _dot(u, wu_ref[:, cols])).astype(BF16)
        h = h + _dot(act, wd_ref[cols, :])
    out_ref[...] = _rms_norm(h, fgain_ref[...]) if final else h


def _ffn_call(h, layer, gain, wg, wu, wd, fgain, final, convert, convert_layer, tm):
    n = h.shape[0]
    row = lambda i: (i, 0)
    cv_in, cv_out, cv_shapes = _convert_specs(convert, convert_layer, n // tm)
    out, *converted = pl.pallas_call(
        functools.partial(_ffn_kernel, final, layer, len(convert)),
        grid=(n // tm,),
        in_specs=[
            pl.BlockSpec((tm, D_MODEL), row),
            _whole_spec(gain),
            _resident_spec(wg),
            _resident_spec(wu),
            _resident_spec(wd),
            pl.BlockSpec((1, D_MODEL), lambda i: (0, 0)),
        ] + cv_in,
        out_specs=[pl.BlockSpec((tm, D_MODEL), row)] + cv_out,
        out_shape=[jax.ShapeDtypeStruct((n, D_MODEL), F32)] + cv_shapes,
        compiler_params=pltpu.CompilerParams(
            dimension_semantics=("arbitrary",), vmem_limit_bytes=VMEM_LIMIT_BYTES),
        name="ffn",
    )(h, gain, wg, wu, wd, fgain, *convert)
    return out, converted


def kernel(x, norm_mix, w_in, w_pool_group, pool_scale, sink, w_pool_branch, w_attn_branch, w_out,
           norm_ffn, w_ffn_gate, w_ffn_up, w_ffn_down, norm_final):
    batch, seq, _ = x.shape
    depth = w_in.shape[0]
    tm_mix, tm_ffn = 1024, 1024
    assert seq % tm_mix == 0 and tm_mix % BLOCK == 0 and (batch * seq) % tm_ffn == 0

    group_shape = w_pool_group.shape[1:]
    mix_f32 = (w_in, w_pool_group.reshape(depth, -1, group_shape[-1]), w_pool_branch, w_attn_branch, w_out)
    ffn_f32 = (w_ffn_gate, w_ffn_up, w_ffn_down)
    mix_b = [w[0].astype(BF16) for w in mix_f32]

    h = x.reshape(batch * seq, D_MODEL)
    for l in range(depth):
        w_in_b, wg_b, wpb_b, wab_b, wo_b = mix_b
        h, ffn_b = _mix_call(h, l, norm_mix, w_in_b, sink, wg_b.reshape(group_shape), pool_scale, wpb_b, wab_b, wo_b,
                             ffn_f32, seq, tm_mix)
        last = l == depth - 1
        h, mix_b = _ffn_call(h, l, norm_ffn, *ffn_b, norm_final[None, :], last, () if last else mix_f32, l + 1,
                             tm_ffn)
    return h.reshape(batch, seq, D_MODEL)
```

```python
import functools

import jax
import jax.numpy as jnp
from jax import lax
from jax.experimental import pallas as pl
from jax.experimental.pallas import tpu as pltpu

D_MODEL = 1024
POOL_WIDTH = 512
POOL_WINDOWS = (2, 4, 8, 16)
POOL_GROUP = POOL_WIDTH // len(POOL_WINDOWS)
N_Q_HEADS = 8
N_KV_HEADS = 2
HEAD_DIM = 64
Q_GROUP = N_Q_HEADS // N_KV_HEADS
Q_WIDTH = N_Q_HEADS * HEAD_DIM
KV_WIDTH = N_KV_HEADS * HEAD_DIM
WINDOW = 128
BLOCK = 128
GATE_WIDTH = 2 * D_MODEL
IN_WIDTH = POOL_WIDTH + Q_WIDTH + 2 * KV_WIDTH + GATE_WIDTH
EPS = 1e-6
LOG2E = 1.4426950408889634

Q_OFF = POOL_WIDTH
KV_OFF = Q_OFF + Q_WIDTH
GATE_OFF = KV_OFF + 2 * KV_WIDTH

LANES = 128
MXU_COLS = 256
FFN_CHUNK = 4 * MXU_COLS
POOL_HALO = 8
BF16_ROWS = 16
assert LANES == 2 * HEAD_DIM and N_KV_HEADS == 2
ALIGNED_HEADS = tuple(h for h in range(N_Q_HEADS) if h % 2 == h // Q_GROUP)
HEAD_ORDER = ALIGNED_HEADS + tuple(h for h in range(N_Q_HEADS) if h not in ALIGNED_HEADS)
VMEM_LIMIT_BYTES = 56 * 1024 * 1024

F32 = jnp.float32
BF16 = jnp.bfloat16


def _rms_norm(x, gain):
    ms = jnp.mean(x * x, axis=-1, keepdims=True)
    return x * lax.rsqrt(ms + EPS) * gain


def _dot(a, b):
    return jnp.dot(a, b, preferred_element_type=F32)


def _dot_tn(a, b):
    return lax.dot_general(a, b, (((0,), (0,)), ((), ())), preferred_element_type=F32)


def _dot_nt(a, b):
    return lax.dot_general(a, b, (((1,), (1,)), ((), ())), preferred_element_type=F32)


def _pool_branch(pos, seq, tm, zext_ref, wg2_ref, scale):
    blocks_per_seq = seq // tm
    h8 = POOL_HALO
    r = lax.broadcasted_iota(jnp.int32, (h8, 1), 0)
    t_first = r
    t_last = seq - h8 + r
    ds = []
    for g, w in enumerate(POOL_WINDOWS):
        cols = slice(g * POOL_GROUP, (g + 1) * POOL_GROUP)
        half = w // 2
        if w == 2:
            s = zext_ref[h8 - 1:h8 - 1 + tm, cols] + zext_ref[h8:h8 + tm, cols]
        else:
            p2 = zext_ref[0:tm + 3 * h8, cols] + zext_ref[1:tm + 3 * h8 + 1, cols]
            if w == 4:
                s = p2[h8 - 2:h8 - 2 + tm] + p2[h8:h8 + tm]
            else:
                p4 = p2[0:tm + 2 * h8] + p2[2:tm + 2 * h8 + 2]
                if w == 8:
                    s = p4[h8 - 4:h8 - 4 + tm] + p4[h8:h8 + tm]
                else:
                    p8 = p4[0:tm + h8] + p4[4:tm + h8 + 4]
                    s = p8[0:tm] + p8[h8:h8 + tm]

        def inv_count(t, at_edge):
            count = (jnp.minimum(t + half, seq) - jnp.maximum(t - half, 0)).astype(F32)
            return jnp.where(at_edge, 1.0 / count, 1.0 / w)

        z = zext_ref[h8:h8 + tm, cols]
        d = jnp.concatenate([
            s[0:h8] * inv_count(t_first, pos == 0) - z[0:h8],
            s[h8:tm - h8] * (1.0 / w) - z[h8:tm - h8],
            s[tm - h8:] * inv_count(t_last, pos == blocks_per_seq - 1) - z[tm - h8:]], axis=0)
        ds.append(d.astype(BF16))
    ys = [_dot(jnp.concatenate(ds[2 * p:2 * p + 2], axis=1), wg2_ref[p]) for p in range(len(ds) // 2)]
    return jnp.concatenate(ys, axis=1) * scale


def _fill_group_pairs(wg_ref, wg2_ref):
    g = POOL_GROUP
    wg2_ref[...] = jnp.zeros(wg2_ref.shape, wg2_ref.dtype)
    for p in range(wg2_ref.shape[0]):
        wg2_ref[p, 0:g, 0:g] = wg_ref[2 * p]
        wg2_ref[p, g:2 * g, g:2 * g] = wg_ref[2 * p + 1]


def _fill_bias(bias_ref):
    c = lax.broadcasted_iota(jnp.int32, (3 * BLOCK, BLOCK), 0)
    a = lax.broadcasted_iota(jnp.int32, (3 * BLOCK, BLOCK), 1)
    absdist = jnp.abs(a - c + BLOCK)
    in_band = absdist <= WINDOW
    absdist_f = absdist.astype(F32)
    for col, h in enumerate(HEAD_ORDER):
        slope = 2.0 ** -(h + 1)
        bias_ref[:, col * BLOCK:(col + 1) * BLOCK] = jnp.where(in_band, (-slope * LOG2E) * absdist_f, -jnp.inf)


def _attention(pos, seq, tm, layer, sink_ref, q, k, k_swapped, v, bias_ref, attn_t_ref, side_work):
    nblk = tm // BLOCK
    lane_half = lax.broadcasted_iota(jnp.int32, (1, LANES), 1) // HEAD_DIM
    sink2 = jnp.concatenate(
        [jnp.full((1, BLOCK), sink_ref[layer, h] * LOG2E, F32) for h in HEAD_ORDER], axis=1)
    edge_first = jnp.where(pos > 0, 0.0, -jnp.inf)
    edge_last = jnp.where(pos < seq // tm - 1, 0.0, -jnp.inf)
    zero = jnp.zeros((), BF16)
    n_aligned = len(ALIGNED_HEADS)
    for n in range(nblk):
        qb = q[n * BLOCK:(n + 1) * BLOCK, :]

        def head_rows(h):
            block = qb[:, (h // 2) * LANES:(h // 2 + 1) * LANES]
            return jnp.where(lane_half == h % 2, block, zero)

        qs = [head_rows(h) for h in HEAD_ORDER]
        rows = slice(n * BLOCK, (n + 3) * BLOCK)
        vb = v[rows]
        s = jnp.concatenate(
            [_dot_nt(k[rows], jnp.concatenate(qs[:n_aligned], axis=0)),
             _dot_nt(k_swapped[rows], jnp.concatenate(qs[n_aligned:], axis=0))], axis=1)
        s = s + bias_ref[...]
        if n == 0:
            s = jnp.concatenate([s[0:BLOCK] + edge_first, s[BLOCK:]], axis=0)
        if n == nblk - 1:
            s = jnp.concatenate([s[:2 * BLOCK], s[2 * BLOCK:] + edge_last], axis=0)
        for work in side_work[n * len(side_work) // nblk:(n + 1) * len(side_work) // nblk]:
            work()
        m = jnp.maximum(jnp.max(s, axis=0, keepdims=True), sink2)
        p = jnp.exp2(s - m)
        denom = jnp.sum(p, axis=0, keepdims=True) + jnp.exp2(sink2 - m)
        p = p.astype(BF16)
        inv = 1.0 / denom
        for kvh in range(N_KV_HEADS):
            cols = [c for c, h in enumerate(HEAD_ORDER) if h // Q_GROUP == kvh]
            p_kv = jnp.concatenate([p[:, c * BLOCK:(c + 1) * BLOCK] for c in cols], axis=1)
            inv_kv = jnp.concatenate([inv[:, c * BLOCK:(c + 1) * BLOCK] for c in cols], axis=1)
            o = _dot_tn(vb[:, kvh * HEAD_DIM:(kvh + 1) * HEAD_DIM], p_kv) * inv_kv
            for j, c in enumerate(cols):
                h = HEAD_ORDER[c]
                attn_t_ref[h * HEAD_DIM:(h + 1) * HEAD_DIM, n * BLOCK:(n + 1) * BLOCK] = (
                    o[:, j * BLOCK:(j + 1) * BLOCK].astype(BF16))


def _convert_slabs(src_refs, dst_refs):
    for src, dst in zip(src_refs, dst_refs):
        dst[...] = src[...].astype(BF16)


def _mix_kernel(seq, tm, layer, n_convert, sink_ref, h_ref, hp_ref, hn_ref, gain_ref, win_ref, wg_ref, ps_ref,
                wpb_ref, wab_ref, wo_ref, *refs):
    convert_src, (out_ref, *convert_dst), scratch = refs[:n_convert], refs[n_convert:2 * n_convert + 1], refs[
        2 * n_convert + 1:]
    zext_ref, bias_ref, wg2_ref, attn_t_ref, gate_ref, pooled_ref, merged_ref = scratch
    _convert_slabs(convert_src, convert_dst)

    @pl.when(pl.program_id(0) == 0)
    def _():
        _fill_bias(bias_ref)
        _fill_group_pairs(wg_ref, wg2_ref)

    blocks_per_seq = seq // tm
    pos = pl.program_id(0) % blocks_per_seq
    gain = gain_ref[layer:layer + 1, :]

    def norm(x):
        return _rms_norm(x, gain).astype(BF16)

    half = tm // 2
    u_lo = jnp.concatenate([norm(hp_ref[...]), norm(h_ref[0:half, :])], axis=0)
    u_hi = jnp.concatenate([norm(h_ref[half:tm, :]), norm(hn_ref[...])], axis=0)
    u = jnp.concatenate([u_lo[BLOCK:], u_hi[:tm - half]], axis=0)

    def project(row_lo, row_hi, col0, width):
        w = win_ref[:, col0:col0 + width]
        return jnp.concatenate([_dot(u_lo[BLOCK + row_lo:], w), _dot(u_hi[:row_hi - half], w)], axis=0)

    zp = project(-BF16_ROWS, tm + BF16_ROWS, 0, POOL_WIDTH)
    h8 = POOL_HALO
    skip = BF16_ROWS - h8
    zext_ref[0:h8, :] = jnp.where(pos > 0, zp[skip:skip + h8], 0.0)
    zext_ref[h8:h8 + tm, :] = zp[BF16_ROWS:BF16_ROWS + tm]
    zext_ref[h8 + tm:2 * h8 + tm, :] = jnp.where(
        pos < blocks_per_seq - 1, zp[BF16_ROWS + tm:BF16_ROWS + tm + h8], 0.0)
    zext_ref[2 * h8 + tm:, :] = jnp.zeros((2 * h8, POOL_WIDTH), F32)

    kv = project(-BLOCK, tm + BLOCK, KV_OFF, 2 * KV_WIDTH)
    k = kv[:, 0:KV_WIDTH].astype(BF16)
    k_swapped = pltpu.roll(kv[:, 0:KV_WIDTH], HEAD_DIM, 1).astype(BF16)
    v = kv[:, KV_WIDTH:2 * KV_WIDTH].astype(BF16)
    q = (project(0, tm, Q_OFF, Q_WIDTH) * (LOG2E * HEAD_DIM ** -0.5)).astype(BF16)

    y = _pool_branch(pos, seq, tm, zext_ref, wg2_ref, ps_ref[layer:layer + 1, :]).astype(BF16)

    def gate(c0):
        return 0.5 + 0.5 * jnp.tanh(0.5 * _dot(u, win_ref[:, GATE_OFF + c0:GATE_OFF + c0 + MXU_COLS]))

    def attn_gate_chunk(c0):
        gate_ref[:, c0:c0 + MXU_COLS] = gate(D_MODEL + c0)

    def pool_chunk(c0):
        cols = slice(c0, c0 + MXU_COLS)
        pooled_ref[:, cols] = gate(c0) * _dot(y, wpb_ref[:, cols])

    chunks = range(0, D_MODEL, MXU_COLS)
    side_work = ([functools.partial(attn_gate_chunk, c0) for c0 in chunks]
                 + [functools.partial(pool_chunk, c0) for c0 in chunks])
    _attention(pos, seq, tm, layer, sink_ref, q, k, k_swapped, v, bias_ref, attn_t_ref, side_work)
    attn_t = attn_t_ref[...]
    for c0 in chunks:
        cols = slice(c0, c0 + MXU_COLS)
        merged = pooled_ref[:, cols] + gate_ref[:, cols] * _dot_tn(attn_t, wab_ref[:, cols])
        merged_ref[:, cols] = merged.astype(BF16)
    out_ref[...] = h_ref[...] + _dot(merged_ref[...], wo_ref[...])


def _resident_spec(arr):
    zeros = (0,) * arr.ndim
    return pl.BlockSpec(arr.shape, lambda i: zeros, pipeline_mode=pl.Buffered(1))


def _whole_spec(arr):
    return pl.BlockSpec(arr.shape, lambda i: (0, 0))


def _convert_specs(weights, layer, n_steps):
    in_specs, out_specs, out_shapes = [], [], []
    for w in weights:
        _, rows, cols = w.shape
        slab = rows // n_steps
        assert rows % n_steps == 0 and slab % BF16_ROWS == 0
        in_specs.append(pl.BlockSpec((None, slab, cols), lambda i: (layer, i, 0)))
        out_specs.append(pl.BlockSpec((slab, cols), lambda i: (i, 0)))
        out_shapes.append(jax.ShapeDtypeStruct((rows, cols), BF16))
    return in_specs, out_specs, out_shapes


def _mix_call(h, layer, gain, w_in, sink, wg, ps, wpb, wab, wo, convert, seq, tm):
    n = h.shape[0]
    row = lambda i: (i, 0)
    per = tm // BLOCK
    prev_blk = lambda i: (jnp.maximum(i * per - 1, 0), 0)
    next_blk = lambda i: (jnp.minimum((i + 1) * per, n // BLOCK - 1), 0)
    cv_in, cv_out, cv_shapes = _convert_specs(convert, layer, n // tm)
    out, *converted = pl.pallas_call(
        functools.partial(_mix_kernel, seq, tm, layer, len(convert)),
        grid=(n // tm,),
        in_specs=[
            pl.BlockSpec(memory_space=pltpu.SMEM),
            pl.BlockSpec((tm, D_MODEL), row),
            pl.BlockSpec((BLOCK, D_MODEL), prev_blk),
            pl.BlockSpec((BLOCK, D_MODEL), next_blk),
            _whole_spec(gain),
            _resident_spec(w_in),
            _resident_spec(wg),
            _whole_spec(ps),
            _resident_spec(wpb),
            _resident_spec(wab),
            _resident_spec(wo),
        ] + cv_in,
        out_specs=[pl.BlockSpec((tm, D_MODEL), row)] + cv_out,
        out_shape=[jax.ShapeDtypeStruct((n, D_MODEL), F32)] + cv_shapes,
        scratch_shapes=[
            pltpu.VMEM((tm + 4 * POOL_HALO, POOL_WIDTH), F32),
            pltpu.VMEM((3 * BLOCK, N_Q_HEADS * BLOCK), F32),
            pltpu.VMEM((len(POOL_WINDOWS) // 2, 2 * POOL_GROUP, 2 * POOL_GROUP), BF16),
            pltpu.VMEM((Q_WIDTH, tm), BF16),
            pltpu.VMEM((tm, D_MODEL), F32),
            pltpu.VMEM((tm, D_MODEL), F32),
            pltpu.VMEM((tm, D_MODEL), BF16),
        ],
        compiler_params=pltpu.CompilerParams(
            dimension_semantics=("arbitrary",), vmem_limit_bytes=VMEM_LIMIT_BYTES),
        name="mix",
    )(sink, h, h, h, gain, w_in, wg, ps, wpb, wab, wo, *convert)
    return out, converted


def _ffn_kernel(final, layer, n_convert, h_ref, gain_ref, wg_ref, wu_ref, wd_ref, fgain_ref, *refs):
    convert_src, (out_ref, *convert_dst) = refs[:n_convert], refs[n_convert:]
    _convert_slabs(convert_src, convert_dst)
    h = h_ref[...]
    u = _rms_norm(h, gain_ref[layer:layer + 1, :]).astype(BF16)
    d_ff = wg_ref.shape[1]
    for c0 in range(0, d_ff, FFN_CHUNK):
        cols = slice(c0, min(c0 + FFN_CHUNK, d_ff))
        gate = _dot(u, wg_ref[:, cols])
        act = ((0.5 * gate) * (1.0 + jnp.tanh(0.5 * gate)) * _dot(u, wu_ref[:, cols])).astype(BF16)
        h = h + _dot(act, wd_ref[cols, :])
    out_ref[...] = _rms_norm(h, fgain_ref[...]) if final else h


def _ffn_call(h, layer, gain, wg, wu, wd, fgain, final, convert, convert_layer, tm):
    n = h.shape[0]
    row = lambda i: (i, 0)
    cv_in, cv_out, cv_shapes = _convert_specs(convert, convert_layer, n // tm)
    out, *converted = pl.pallas_call(
        functools.partial(_ffn_kernel, final, layer, len(convert)),
        grid=(n // tm,),
        in_specs=[
            pl.BlockSpec((tm, D_MODEL), row),
            _whole_spec(gain),
            _resident_spec(wg),
            _resident_spec(wu),
            _resident_spec(wd),
            pl.BlockSpec((1, D_MODEL), lambda i: (0, 0)),
        ] + cv_in,
        out_specs=[pl.BlockSpec((tm, D_MODEL), row)] + cv_out,
        out_shape=[jax.ShapeDtypeStruct((n, D_MODEL), F32)] + cv_shapes,
        compiler_params=pltpu.CompilerParams(
            dimension_semantics=("arbitrary",), vmem_limit_bytes=VMEM_LIMIT_BYTES),
        name="ffn",
    )(h, gain, wg, wu, wd, fgain, *convert)
    return out, converted


def kernel(x, norm_mix, w_in, w_pool_group, pool_scale, sink, w_pool_branch, w_attn_branch, w_out,
           norm_ffn, w_ffn_gate, w_ffn_up, w_ffn_down, norm_final):
    batch, seq, _ = x.shape
    depth = w_in.shape[0]
    tm_mix, tm_ffn = 1024, 1024
    assert seq % tm_mix == 0 and tm_mix % BLOCK == 0 and (batch * seq) % tm_ffn == 0

    group_shape = w_pool_group.shape[1:]
    mix_f32 = (w_in, w_pool_group.reshape(depth, -1, group_shape[-1]), w_pool_branch, w_attn_branch, w_out)
    ffn_f32 = (w_ffn_gate, w_ffn_up, w_ffn_down)
    mix_b = [w[0].astype(BF16) for w in mix_f32]

    h = x.reshape(batch * seq, D_MODEL)
    for l in range(depth):
        w_in_b, wg_b, wpb_b, wab_b, wo_b = mix_b
        h, ffn_b = _mix_call(h, l, norm_mix, w_in_b, sink, wg_b.reshape(group_shape), pool_scale, wpb_b, wab_b, wo_b,
                             ffn_f32, seq, tm_mix)
        last = l == depth - 1
        h, mix_b = _ffn_call(h, l, norm_ffn, *ffn_b, norm_final[None, :], last, () if last else mix_f32, l + 1,
                             tm_ffn)
    return h.reshape(batch, seq, D_MODEL)
```

```python
import functools

import jax
import jax.numpy as jnp
from jax import lax
from jax.experimental import pallas as pl
from jax.experimental.pallas import tpu as pltpu

D_MODEL = 1024
POOL_WIDTH = 512
POOL_WINDOWS = (2, 4, 8, 16)
POOL_GROUP = POOL_WIDTH // len(POOL_WINDOWS)
N_Q_HEADS = 8
N_KV_HEADS = 2
HEAD_DIM = 64
Q_GROUP = N_Q_HEADS // N_KV_HEADS
Q_WIDTH = N_Q_HEADS * HEAD_DIM
KV_WIDTH = N_KV_HEADS * HEAD_DIM
WINDOW = 128
BLOCK = 128
GATE_WIDTH = 2 * D_MODEL
IN_WIDTH = POOL_WIDTH + Q_WIDTH + 2 * KV_WIDTH + GATE_WIDTH
EPS = 1e-6
LOG2E = 1.4426950408889634

Q_OFF = POOL_WIDTH
KV_OFF = Q_OFF + Q_WIDTH
GATE_OFF = KV_OFF + 2 * KV_WIDTH

LANES = 128
MXU_COLS = 256
FFN_CHUNK = 4 * MXU_COLS
CONVERT_STEPS = 4
POOL_HALO = 8
BF16_ROWS = 16
assert LANES == 2 * HEAD_DIM and N_KV_HEADS == 2
ALIGNED_HEADS = tuple(h for h in range(N_Q_HEADS) if h % 2 == h // Q_GROUP)
HEAD_ORDER = ALIGNED_HEADS + tuple(h for h in range(N_Q_HEADS) if h not in ALIGNED_HEADS)
VMEM_LIMIT_BYTES = 56 * 1024 * 1024

F32 = jnp.float32
BF16 = jnp.bfloat16


def _rms_norm(x, gain):
    ms = jnp.mean(x * x, axis=-1, keepdims=True)
    return x * lax.rsqrt(ms + EPS) * gain


def _dot(a, b):
    return jnp.dot(a, b, preferred_element_type=F32)


def _dot_tn(a, b):
    return lax.dot_general(a, b, (((0,), (0,)), ((), ())), preferred_element_type=F32)


def _dot_nt(a, b):
    return lax.dot_general(a, b, (((1,), (1,)), ((), ())), preferred_element_type=F32)


def _pool_branch(pos, seq, tm, zext_ref, wg2_ref, scale):
    blocks_per_seq = seq // tm
    h8 = POOL_HALO
    r = lax.broadcasted_iota(jnp.int32, (h8, 1), 0)
    t_first = r
    t_last = seq - h8 + r
    ds = []
    for g, w in enumerate(POOL_WINDOWS):
        cols = slice(g * POOL_GROUP, (g + 1) * POOL_GROUP)
        half = w // 2
        if w == 2:
            s = zext_ref[h8 - 1:h8 - 1 + tm, cols] + zext_ref[h8:h8 + tm, cols]
        else:
            p2 = zext_ref[0:tm + 3 * h8, cols] + zext_ref[1:tm + 3 * h8 + 1, cols]
            if w == 4:
                s = p2[h8 - 2:h8 - 2 + tm] + p2[h8:h8 + tm]
            else:
                p4 = p2[0:tm + 2 * h8] + p2[2:tm + 2 * h8 + 2]
                if w == 8:
                    s = p4[h8 - 4:h8 - 4 + tm] + p4[h8:h8 + tm]
                else:
                    p8 = p4[0:tm + h8] + p4[4:tm + h8 + 4]
                    s = p8[0:tm] + p8[h8:h8 + tm]

        def inv_count(t, at_edge):
            count = (jnp.minimum(t + half, seq) - jnp.maximum(t - half, 0)).astype(F32)
            return jnp.where(at_edge, 1.0 / count, 1.0 / w)

        z = zext_ref[h8:h8 + tm, cols]
        d = jnp.concatenate([
            s[0:h8] * inv_count(t_first, pos == 0) - z[0:h8],
            s[h8:tm - h8] * (1.0 / w) - z[h8:tm - h8],
            s[tm - h8:] * inv_count(t_last, pos == blocks_per_seq - 1) - z[tm - h8:]], axis=0)
        ds.append(d.astype(BF16))
    ys = [_dot(jnp.concatenate(ds[2 * p:2 * p + 2], axis=1), wg2_ref[p]) for p in range(len(ds) // 2)]
    return jnp.concatenate(ys, axis=1) * scale


def _fill_group_pairs(wg_ref, wg2_ref):
    g = POOL_GROUP
    wg2_ref[...] = jnp.zeros(wg2_ref.shape, wg2_ref.dtype)
    for p in range(wg2_ref.shape[0]):
        wg2_ref[p, 0:g, 0:g] = wg_ref[2 * p]
        wg2_ref[p, g:2 * g, g:2 * g] = wg_ref[2 * p + 1]


def _fill_bias(bias_ref):
    c = lax.broadcasted_iota(jnp.int32, (3 * BLOCK, BLOCK), 0)
    a = lax.broadcasted_iota(jnp.int32, (3 * BLOCK, BLOCK), 1)
    absdist = jnp.abs(a - c + BLOCK)
    in_band = absdist <= WINDOW
    absdist_f = absdist.astype(F32)
    for col, h in enumerate(HEAD_ORDER):
        slope = 2.0 ** -(h + 1)
        bias_ref[:, col * BLOCK:(col + 1) * BLOCK] = jnp.where(in_band, (-slope * LOG2E) * absdist_f, -jnp.inf)


def _attention(pos, seq, tm, layer, sink_ref, q, k, k_swapped, v, bias_ref, attn_t_ref, side_work):
    nblk = tm // BLOCK
    lane_half = lax.broadcasted_iota(jnp.int32, (1, LANES), 1) // HEAD_DIM
    sink2 = jnp.concatenate(
        [jnp.full((1, BLOCK), sink_ref[layer, h] * LOG2E, F32) for h in HEAD_ORDER], axis=1)
    edge_first = jnp.where(pos > 0, 0.0, -jnp.inf)
    edge_last = jnp.where(pos < seq // tm - 1, 0.0, -jnp.inf)
    zero = jnp.zeros((), BF16)
    n_aligned = len(ALIGNED_HEADS)
    for n in range(nblk):
        qb = q[n * BLOCK:(n + 1) * BLOCK, :]

        def head_rows(h):
            block = qb[:, (h // 2) * LANES:(h // 2 + 1) * LANES]
            return jnp.where(lane_half == h % 2, block, zero)

        qs = [head_rows(h) for h in HEAD_ORDER]
        rows = slice(n * BLOCK, (n + 3) * BLOCK)
        vb = v[rows]
        s = jnp.concatenate(
            [_dot_nt(k[rows], jnp.concatenate(qs[:n_aligned], axis=0)),
             _dot_nt(k_swapped[rows], jnp.concatenate(qs[n_aligned:], axis=0))], axis=1)
        s = s + bias_ref[...]
        if n == 0:
            s = jnp.concatenate([s[0:BLOCK] + edge_first, s[BLOCK:]], axis=0)
        if n == nblk - 1:
            s = jnp.concatenate([s[:2 * BLOCK], s[2 * BLOCK:] + edge_last], axis=0)
        for work in side_work[n * len(side_work) // nblk:(n + 1) * len(side_work) // nblk]:
            work()
        m = jnp.maximum(jnp.max(s, axis=0, keepdims=True), sink2)
        p = jnp.exp2(s - m)
        denom = jnp.sum(p, axis=0, keepdims=True) + jnp.exp2(sink2 - m)
        p = p.astype(BF16)
        inv = 1.0 / denom
        for kvh in range(N_KV_HEADS):
            cols = [c for c, h in enumerate(HEAD_ORDER) if h // Q_GROUP == kvh]
            p_kv = jnp.concatenate([p[:, c * BLOCK:(c + 1) * BLOCK] for c in cols], axis=1)
            inv_kv = jnp.concatenate([inv[:, c * BLOCK:(c + 1) * BLOCK] for c in cols], axis=1)
            o = _dot_tn(vb[:, kvh * HEAD_DIM:(kvh + 1) * HEAD_DIM], p_kv) * inv_kv
            for j, c in enumerate(cols):
                h = HEAD_ORDER[c]
                attn_t_ref[h * HEAD_DIM:(h + 1) * HEAD_DIM, n * BLOCK:(n + 1) * BLOCK] = (
                    o[:, j * BLOCK:(j + 1) * BLOCK].astype(BF16))


def _convert_slabs(src_refs, dst_refs):
    for src, dst in zip(src_refs, dst_refs):
        dst[...] = src[...].astype(BF16)


def _mix_kernel(seq, tm, layer, n_convert, sink_ref, h_ref, hp_ref, hn_ref, gain_ref, win_ref, wg_ref, ps_ref,
                wpb_ref, wab_ref, wo_ref, *refs):
    convert_src, (out_ref, *convert_dst), scratch = refs[:n_convert], refs[n_convert:2 * n_convert + 1], refs[
        2 * n_convert + 1:]
    zext_ref, bias_ref, wg2_ref, attn_t_ref, gate_ref, pooled_ref, merged_ref = scratch
    _convert_slabs(convert_src, convert_dst)

    @pl.when(pl.program_id(0) == 0)
    def _():
        _fill_bias(bias_ref)
        _fill_group_pairs(wg_ref, wg2_ref)

    blocks_per_seq = seq // tm
    pos = pl.program_id(0) % blocks_per_seq
    gain = gain_ref[layer:layer + 1, :]

    def norm(x):
        return _rms_norm(x, gain).astype(BF16)

    half = tm // 2
    u_lo = jnp.concatenate([norm(hp_ref[...]), norm(h_ref[0:half, :])], axis=0)
    u_hi = jnp.concatenate([norm(h_ref[half:tm, :]), norm(hn_ref[...])], axis=0)
    u = jnp.concatenate([u_lo[BLOCK:], u_hi[:tm - half]], axis=0)

    def project(row_lo, row_hi, col0, width):
        w = win_ref[:, col0:col0 + width]
        return jnp.concatenate([_dot(u_lo[BLOCK + row_lo:], w), _dot(u_hi[:row_hi - half], w)], axis=0)

    zp = project(-BF16_ROWS, tm + BF16_ROWS, 0, POOL_WIDTH)
    h8 = POOL_HALO
    skip = BF16_ROWS - h8
    zext_ref[0:h8, :] = jnp.where(pos > 0, zp[skip:skip + h8], 0.0)
    zext_ref[h8:h8 + tm, :] = zp[BF16_ROWS:BF16_ROWS + tm]
    zext_ref[h8 + tm:2 * h8 + tm, :] = jnp.where(
        pos < blocks_per_seq - 1, zp[BF16_ROWS + tm:BF16_ROWS + tm + h8], 0.0)
    zext_ref[2 * h8 + tm:, :] = jnp.zeros((2 * h8, POOL_WIDTH), F32)

    kv = project(-BLOCK, tm + BLOCK, KV_OFF, 2 * KV_WIDTH)
    k = kv[:, 0:KV_WIDTH].astype(BF16)
    k_swapped = pltpu.roll(kv[:, 0:KV_WIDTH], HEAD_DIM, 1).astype(BF16)
    v = kv[:, KV_WIDTH:2 * KV_WIDTH].astype(BF16)
    q = (project(0, tm, Q_OFF, Q_WIDTH) * (LOG2E * HEAD_DIM ** -0.5)).astype(BF16)

    y = _pool_branch(pos, seq, tm, zext_ref, wg2_ref, ps_ref[layer:layer + 1, :]).astype(BF16)

    def gate(c0):
        return 0.5 + 0.5 * jnp.tanh(0.5 * _dot(u, win_ref[:, GATE_OFF + c0:GATE_OFF + c0 + MXU_COLS]))

    def attn_gate_chunk(c0):
        gate_ref[:, c0:c0 + MXU_COLS] = gate(D_MODEL + c0)

    def pool_chunk(c0):
        cols = slice(c0, c0 + MXU_COLS)
        pooled_ref[:, cols] = gate(c0) * _dot(y, wpb_ref[:, cols])

    chunks = range(0, D_MODEL, MXU_COLS)
    side_work = ([functools.partial(attn_gate_chunk, c0) for c0 in chunks]
                 + [functools.partial(pool_chunk, c0) for c0 in chunks])
    _attention(pos, seq, tm, layer, sink_ref, q, k, k_swapped, v, bias_ref, attn_t_ref, side_work)
    attn_t = attn_t_ref[...]
    for c0 in chunks:
        cols = slice(c0, c0 + MXU_COLS)
        merged = pooled_ref[:, cols] + gate_ref[:, cols] * _dot_tn(attn_t, wab_ref[:, cols])
        merged_ref[:, cols] = merged.astype(BF16)
    out_ref[...] = h_ref[...] + _dot(merged_ref[...], wo_ref[...])


def _resident_spec(arr):
    zeros = (0,) * arr.ndim
    return pl.BlockSpec(arr.shape, lambda i: zeros, pipeline_mode=pl.Buffered(1))


def _whole_spec(arr):
    return pl.BlockSpec(arr.shape, lambda i: (0, 0))


def _convert_specs(weights, layer, n_steps):
    in_specs, out_specs, out_shapes = [], [], []
    for w in weights:
        _, rows, cols = w.shape
        slab = rows // n_steps
        assert rows % n_steps == 0 and slab % BF16_ROWS == 0
        in_specs.append(pl.BlockSpec((None, slab, cols), lambda i: (layer, i, 0)))
        out_specs.append(pl.BlockSpec((slab, cols), lambda i: (i, 0)))
        out_shapes.append(jax.ShapeDtypeStruct((rows, cols), BF16))
    return in_specs, out_specs, out_shapes


def _mix_call(h, layer, gain, w_in, sink, wg, ps, wpb, wab, wo, convert, seq, tm):
    n = h.shape[0]
    row = lambda i: (i, 0)
    per = tm // BLOCK
    prev_blk = lambda i: (jnp.maximum(i * per - 1, 0), 0)
    next_blk = lambda i: (jnp.minimum((i + 1) * per, n // BLOCK - 1), 0)
    cv_in, cv_out, cv_shapes = _convert_specs(convert, layer, n // tm)
    out, *converted = pl.pallas_call(
        functools.partial(_mix_kernel, seq, tm, layer, len(convert)),
        grid=(n // tm,),
        in_specs=[
            pl.BlockSpec(memory_space=pltpu.SMEM),
            pl.BlockSpec((tm, D_MODEL), row),
            pl.BlockSpec((BLOCK, D_MODEL), prev_blk),
            pl.BlockSpec((BLOCK, D_MODEL), next_blk),
            _whole_spec(gain),
            _resident_spec(w_in),
            _resident_spec(wg),
            _whole_spec(ps),
            _resident_spec(wpb),
            _resident_spec(wab),
            _resident_spec(wo),
        ] + cv_in,
        out_specs=[pl.BlockSpec((tm, D_MODEL), row)] + cv_out,
        out_shape=[jax.ShapeDtypeStruct((n, D_MODEL), F32)] + cv_shapes,
        scratch_shapes=[
            pltpu.VMEM((tm + 4 * POOL_HALO, POOL_WIDTH), F32),
            pltpu.VMEM((3 * BLOCK, N_Q_HEADS * BLOCK), F32),
            pltpu.VMEM((len(POOL_WINDOWS) // 2, 2 * POOL_GROUP, 2 * POOL_GROUP), BF16),
            pltpu.VMEM((Q_WIDTH, tm), BF16),
            pltpu.VMEM((tm, D_MODEL), F32),
            pltpu.VMEM((tm, D_MODEL), F32),
            pltpu.VMEM((tm, D_MODEL), BF16),
        ],
        compiler_params=pltpu.CompilerParams(
            dimension_semantics=("arbitrary",), vmem_limit_bytes=VMEM_LIMIT_BYTES),
        name="mix",
    )(sink, h, h, h, gain, w_in, wg, ps, wpb, wab, wo, *convert)
    return out, converted


def _ffn_kernel(final, layer, n_convert, h_ref, gain_ref, wg_ref, wu_ref, wd_ref, fgain_ref, *refs):
    convert_src, (out_ref, *convert_dst) = refs[:n_convert], refs[n_convert:]
    _convert_slabs(convert_src, convert_dst)
    h = h_ref[...]
    u = _rms_norm(h, gain_ref[layer:layer + 1, :]).astype(BF16)
    d_ff = wg_ref.shape[1]
    for c0 in range(0, d_ff, FFN_CHUNK):
        cols = slice(c0, min(c0 + FFN_CHUNK, d_ff))
        gate = _dot(u, wg_ref[:, cols])
        act = ((0.5 * gate) * (1.0 + jnp.tanh(0.5 * gate)) * _dot(u, wu_ref[:, cols])).astype(BF16)
        h = h + _dot(act, wd_ref[cols, :])
    out_ref[...] = _rms_norm(h, fgain_ref[...]) if final else h


def _ffn_call(h, layer, gain, wg, wu, wd, fgain, final, convert, convert_layer, tm):
    n = h.shape[0]
    row = lambda i: (i, 0)
    cv_in, cv_out, cv_shapes = _convert_specs(convert, convert_layer, n // tm)
    out, *converted = pl.pallas_call(
        functools.partial(_ffn_kernel, final, layer, len(convert)),
        grid=(n // tm,),
        in_specs=[
            pl.BlockSpec((tm, D_MODEL), row),
            _whole_spec(gain),
            _resident_spec(wg),
            _resident_spec(wu),
            _resident_spec(wd),
            pl.BlockSpec((1, D_MODEL), lambda i: (0, 0)),
        ] + cv_in,
        out_specs=[pl.BlockSpec((tm, D_MODEL), row)] + cv_out,
        out_shape=[jax.ShapeDtypeStruct((n, D_MODEL), F32)] + cv_shapes,
        compiler_params=pltpu.CompilerParams(
            dimension_semantics=("arbitrary",), vmem_limit_bytes=VMEM_LIMIT_BYTES),
        name="ffn",
    )(h, gain, wg, wu, wd, fgain, *convert)
    return out, converted


def _convert_kernel(n_convert, *refs):
    _convert_slabs(refs[:n_convert], refs[n_convert:])


def _convert_call(weights, layer):
    cv_in, cv_out, cv_shapes = _convert_specs(weights, layer, CONVERT_STEPS)
    return pl.pallas_call(
        functools.partial(_convert_kernel, len(weights)),
        grid=(CONVERT_STEPS,),
        in_specs=cv_in,
        out_specs=cv_out,
        out_shape=cv_shapes,
        compiler_params=pltpu.CompilerParams(
            dimension_semantics=("arbitrary",), vmem_limit_bytes=VMEM_LIMIT_BYTES),
        name="convert",
    )(*weights)


def kernel(x, norm_mix, w_in, w_pool_group, pool_scale, sink, w_pool_branch, w_attn_branch, w_out,
           norm_ffn, w_ffn_gate, w_ffn_up, w_ffn_down, norm_final):
    batch, seq, _ = x.shape
    depth = w_in.shape[0]
    tm_mix, tm_ffn = 1024, 1024
    assert seq % tm_mix == 0 and tm_mix % BLOCK == 0 and (batch * seq) % tm_ffn == 0

    group_shape = w_pool_group.shape[1:]
    mix_f32 = (w_in, w_pool_group.reshape(depth, -1, group_shape[-1]), w_pool_branch, w_attn_branch, w_out)
    ffn_f32 = (w_ffn_gate, w_ffn_up, w_ffn_down)
    mix_b = _convert_call(mix_f32, 0)

    h = x.reshape(batch * seq, D_MODEL)
    for l in range(depth):
        w_in_b, wg_b, wpb_b, wab_b, wo_b = mix_b
        h, ffn_b = _mix_call(h, l, norm_mix, w_in_b, sink, wg_b.reshape(group_shape), pool_scale, wpb_b, wab_b, wo_b,
                             ffn_f32, seq, tm_mix)
        last = l == depth - 1
        h, mix_b = _ffn_call(h, l, norm_ffn, *ffn_b, norm_final[None, :], last, () if last else mix_f32, l + 1,
                             tm_ffn)
    return h.reshape(batch, seq, D_MODEL)
```

```python
import functools

import jax
import jax.numpy as jnp
from jax import lax
from jax.experimental import pallas as pl
from jax.experimental.pallas import tpu as pltpu

D_MODEL = 1024
POOL_WIDTH = 512
POOL_WINDOWS = (2, 4, 8, 16)
POOL_GROUP = POOL_WIDTH // len(POOL_WINDOWS)
N_Q_HEADS = 8
N_KV_HEADS = 2
HEAD_DIM = 64
Q_GROUP = N_Q_HEADS // N_KV_HEADS
Q_WIDTH = N_Q_HEADS * HEAD_DIM
KV_WIDTH = N_KV_HEADS * HEAD_DIM
WINDOW = 128
BLOCK = 128
GATE_WIDTH = 2 * D_MODEL
IN_WIDTH = POOL_WIDTH + Q_WIDTH + 2 * KV_WIDTH + GATE_WIDTH
EPS = 1e-6
LOG2E = 1.4426950408889634

Q_OFF = POOL_WIDTH
KV_OFF = Q_OFF + Q_WIDTH
GATE_OFF = KV_OFF + 2 * KV_WIDTH

LANES = 128
MXU_COLS = 256
FFN_CHUNK = 4 * MXU_COLS
CONVERT_STEPS = 4
POOL_HALO = 8
BF16_ROWS = 16
assert LANES == 2 * HEAD_DIM and N_KV_HEADS == 2
ALIGNED_HEADS = tuple(h for h in range(N_Q_HEADS) if h % 2 == h // Q_GROUP)
HEAD_ORDER = ALIGNED_HEADS + tuple(h for h in range(N_Q_HEADS) if h not in ALIGNED_HEADS)
VMEM_LIMIT_BYTES = 58 * 1024 * 1024

F32 = jnp.float32
BF16 = jnp.bfloat16


def _rms_norm(x, gain):
    ms = jnp.mean(x * x, axis=-1, keepdims=True)
    return x * lax.rsqrt(ms + EPS) * gain


def _dot(a, b):
    return jnp.dot(a, b, preferred_element_type=F32)


def _dot_tn(a, b):
    return lax.dot_general(a, b, (((0,), (0,)), ((), ())), preferred_element_type=F32)


def _dot_nt(a, b):
    return lax.dot_general(a, b, (((1,), (1,)), ((), ())), preferred_element_type=F32)


def _pool_branch(pos, seq, tm, zext_ref, wg2_ref, scale):
    blocks_per_seq = seq // tm
    h8 = POOL_HALO
    r = lax.broadcasted_iota(jnp.int32, (h8, 1), 0)
    t_first = r
    t_last = seq - h8 + r
    ds = []
    for g, w in enumerate(POOL_WINDOWS):
        cols = slice(g * POOL_GROUP, (g + 1) * POOL_GROUP)
        half = w // 2
        if w == 2:
            s = zext_ref[h8 - 1:h8 - 1 + tm, cols] + zext_ref[h8:h8 + tm, cols]
        else:
            p2 = zext_ref[0:tm + 3 * h8, cols] + zext_ref[1:tm + 3 * h8 + 1, cols]
            if w == 4:
                s = p2[h8 - 2:h8 - 2 + tm] + p2[h8:h8 + tm]
            else:
                p4 = p2[0:tm + 2 * h8] + p2[2:tm + 2 * h8 + 2]
                if w == 8:
                    s = p4[h8 - 4:h8 - 4 + tm] + p4[h8:h8 + tm]
                else:
                    p8 = p4[0:tm + h8] + p4[4:tm + h8 + 4]
                    s = p8[0:tm] + p8[h8:h8 + tm]

        def inv_count(t, at_edge):
            count = (jnp.minimum(t + half, seq) - jnp.maximum(t - half, 0)).astype(F32)
            return jnp.where(at_edge, 1.0 / count, 1.0 / w)

        z = zext_ref[h8:h8 + tm, cols]
        d = jnp.concatenate([
            s[0:h8] * inv_count(t_first, pos == 0) - z[0:h8],
            s[h8:tm - h8] * (1.0 / w) - z[h8:tm - h8],
            s[tm - h8:] * inv_count(t_last, pos == blocks_per_seq - 1) - z[tm - h8:]], axis=0)
        ds.append(d.astype(BF16))
    ys = [_dot(jnp.concatenate(ds[2 * p:2 * p + 2], axis=1), wg2_ref[p]) for p in range(len(ds) // 2)]
    return jnp.concatenate(ys, axis=1) * scale


def _fill_group_pairs(wg_ref, wg2_ref):
    g = POOL_GROUP
    wg2_ref[...] = jnp.zeros(wg2_ref.shape, wg2_ref.dtype)
    for p in range(wg2_ref.shape[0]):
        wg2_ref[p, 0:g, 0:g] = wg_ref[2 * p]
        wg2_ref[p, g:2 * g, g:2 * g] = wg_ref[2 * p + 1]


def _fill_bias(bias_ref):
    c = lax.broadcasted_iota(jnp.int32, (3 * BLOCK, BLOCK), 0)
    a = lax.broadcasted_iota(jnp.int32, (3 * BLOCK, BLOCK), 1)
    absdist = jnp.abs(a - c + BLOCK)
    in_band = absdist <= WINDOW
    absdist_f = absdist.astype(F32)
    for col, h in enumerate(HEAD_ORDER):
        slope = 2.0 ** -(h + 1)
        bias_ref[:, col * BLOCK:(col + 1) * BLOCK] = jnp.where(in_band, (-slope * LOG2E) * absdist_f, -jnp.inf)


def _attention(pos, seq, tm, layer, sink_ref, q, k, k_swapped, v, bias_ref, attn_t_ref, side_work):
    nblk = tm // BLOCK
    lane_half = lax.broadcasted_iota(jnp.int32, (1, LANES), 1) // HEAD_DIM
    sink2 = jnp.concatenate(
        [jnp.full((1, BLOCK), sink_ref[layer, h] * LOG2E, F32) for h in HEAD_ORDER], axis=1)
    edge_first = jnp.where(pos > 0, 0.0, -jnp.inf)
    edge_last = jnp.where(pos < seq // tm - 1, 0.0, -jnp.inf)
    zero = jnp.zeros((), BF16)
    n_aligned = len(ALIGNED_HEADS)
    for n in range(nblk):
        qb = q[n * BLOCK:(n + 1) * BLOCK, :]

        def head_rows(h):
            block = qb[:, (h // 2) * LANES:(h // 2 + 1) * LANES]
            return jnp.where(lane_half == h % 2, block, zero)

        qs = [head_rows(h) for h in HEAD_ORDER]
        rows = slice(n * BLOCK, (n + 3) * BLOCK)
        vb = v[rows]
        s = jnp.concatenate(
            [_dot_nt(k[rows], jnp.concatenate(qs[:n_aligned], axis=0)),
             _dot_nt(k_swapped[rows], jnp.concatenate(qs[n_aligned:], axis=0))], axis=1)
        s = s + bias_ref[...]
        if n == 0:
            s = jnp.concatenate([s[0:BLOCK] + edge_first, s[BLOCK:]], axis=0)
        if n == nblk - 1:
            s = jnp.concatenate([s[:2 * BLOCK], s[2 * BLOCK:] + edge_last], axis=0)
        for work in side_work[n * len(side_work) // nblk:(n + 1) * len(side_work) // nblk]:
            work()
        m = jnp.maximum(jnp.max(s, axis=0, keepdims=True), sink2)
        p = jnp.exp2(s - m)
        denom = jnp.sum(p, axis=0, keepdims=True) + jnp.exp2(sink2 - m)
        p = p.astype(BF16)
        inv = 1.0 / denom
        for kvh in range(N_KV_HEADS):
            cols = [c for c, h in enumerate(HEAD_ORDER) if h // Q_GROUP == kvh]
            p_kv = jnp.concatenate([p[:, c * BLOCK:(c + 1) * BLOCK] for c in cols], axis=1)
            inv_kv = jnp.concatenate([inv[:, c * BLOCK:(c + 1) * BLOCK] for c in cols], axis=1)
            o = _dot_tn(vb[:, kvh * HEAD_DIM:(kvh + 1) * HEAD_DIM], p_kv) * inv_kv
            for j, c in enumerate(cols):
                h = HEAD_ORDER[c]
                attn_t_ref[h * HEAD_DIM:(h + 1) * HEAD_DIM, n * BLOCK:(n + 1) * BLOCK] = (
                    o[:, j * BLOCK:(j + 1) * BLOCK].astype(BF16))


def _convert_slabs(src_refs, dst_refs):
    for src, dst in zip(src_refs, dst_refs):
        dst[...] = src[...].astype(BF16)


def _mix_kernel(seq, tm, layer, n_convert, sink_ref, h_ref, hp_ref, hn_ref, gain_ref, win_ref, wg_ref, ps_ref,
                wpb_ref, wab_ref, wo_ref, *refs):
    convert_src, (out_ref, *convert_dst), scratch = refs[:n_convert], refs[n_convert:2 * n_convert + 1], refs[
        2 * n_convert + 1:]
    zext_ref, bias_ref, wg2_ref, attn_t_ref, gate_ref, pooled_ref, merged_ref = scratch
    _convert_slabs(convert_src, convert_dst)

    @pl.when(pl.program_id(0) == 0)
    def _():
        _fill_bias(bias_ref)
        _fill_group_pairs(wg_ref, wg2_ref)

    blocks_per_seq = seq // tm
    pos = pl.program_id(0) % blocks_per_seq
    gain = gain_ref[layer:layer + 1, :]

    def norm(x):
        return _rms_norm(x, gain).astype(BF16)

    half = tm // 2
    u_lo = jnp.concatenate([norm(hp_ref[...]), norm(h_ref[0:half, :])], axis=0)
    u_hi = jnp.concatenate([norm(h_ref[half:tm, :]), norm(hn_ref[...])], axis=0)
    u = jnp.concatenate([u_lo[BLOCK:], u_hi[:tm - half]], axis=0)

    def project(row_lo, row_hi, col0, width):
        w = win_ref[:, col0:col0 + width]
        return jnp.concatenate([_dot(u_lo[BLOCK + row_lo:], w), _dot(u_hi[:row_hi - half], w)], axis=0)

    zp = project(-BF16_ROWS, tm + BF16_ROWS, 0, POOL_WIDTH)
    h8 = POOL_HALO
    skip = BF16_ROWS - h8
    zext_ref[0:h8, :] = jnp.where(pos > 0, zp[skip:skip + h8], 0.0)
    zext_ref[h8:h8 + tm, :] = zp[BF16_ROWS:BF16_ROWS + tm]
    zext_ref[h8 + tm:2 * h8 + tm, :] = jnp.where(
        pos < blocks_per_seq - 1, zp[BF16_ROWS + tm:BF16_ROWS + tm + h8], 0.0)
    zext_ref[2 * h8 + tm:, :] = jnp.zeros((2 * h8, POOL_WIDTH), F32)

    kv = project(-BLOCK, tm + BLOCK, KV_OFF, 2 * KV_WIDTH)
    k = kv[:, 0:KV_WIDTH].astype(BF16)
    k_swapped = pltpu.roll(kv[:, 0:KV_WIDTH], HEAD_DIM, 1).astype(BF16)
    v = kv[:, KV_WIDTH:2 * KV_WIDTH].astype(BF16)
    q = (project(0, tm, Q_OFF, Q_WIDTH) * (LOG2E * HEAD_DIM ** -0.5)).astype(BF16)

    pooled_feats = []

    def gate(c0):
        return 0.5 + 0.5 * jnp.tanh(0.5 * _dot(u, win_ref[:, GATE_OFF + c0:GATE_OFF + c0 + MXU_COLS]))

    def attn_gate_chunk(c0):
        gate_ref[:, c0:c0 + MXU_COLS] = gate(D_MODEL + c0)

    def pool_chunk(c0):
        if not pooled_feats:
            pooled_feats.append(
                _pool_branch(pos, seq, tm, zext_ref, wg2_ref, ps_ref[layer:layer + 1, :]).astype(BF16))
        cols = slice(c0, c0 + MXU_COLS)
        pooled_ref[:, cols] = gate(c0) * _dot(pooled_feats[0], wpb_ref[:, cols])

    chunks = range(0, D_MODEL, MXU_COLS)
    side_work = ([functools.partial(attn_gate_chunk, c0) for c0 in chunks]
                 + [functools.partial(pool_chunk, c0) for c0 in chunks])
    _attention(pos, seq, tm, layer, sink_ref, q, k, k_swapped, v, bias_ref, attn_t_ref, side_work)
    attn_t = attn_t_ref[...]
    for c0 in chunks:
        cols = slice(c0, c0 + MXU_COLS)
        merged = pooled_ref[:, cols] + gate_ref[:, cols] * _dot_tn(attn_t, wab_ref[:, cols])
        merged_ref[:, cols] = merged.astype(BF16)
    out_ref[...] = h_ref[...] + _dot(merged_ref[...], wo_ref[...])


def _resident_spec(arr):
    zeros = (0,) * arr.ndim
    return pl.BlockSpec(arr.shape, lambda i: zeros, pipeline_mode=pl.Buffered(1))


def _whole_spec(arr):
    return pl.BlockSpec(arr.shape, lambda i: (0, 0))


def _convert_specs(weights, layer, n_steps):
    in_specs, out_specs, out_shapes = [], [], []
    for w in weights:
        _, rows, cols = w.shape
        slab = rows // n_steps
        assert rows % n_steps == 0 and slab % BF16_ROWS == 0
        in_specs.append(pl.BlockSpec((None, slab, cols), lambda i: (layer, jnp.minimum(i, n_steps - 1), 0)))
        out_specs.append(pl.BlockSpec((slab, cols), lambda i: (jnp.minimum(i, n_steps - 1), 0)))
        out_shapes.append(jax.ShapeDtypeStruct((rows, cols), BF16))
    return in_specs, out_specs, out_shapes


def _mix_call(h, layer, gain, w_in, sink, wg, ps, wpb, wab, wo, convert, seq, tm):
    n = h.shape[0]
    row = lambda i: (i, 0)
    per = tm // BLOCK
    prev_blk = lambda i: (jnp.maximum(i * per - 1, 0), 0)
    next_blk = lambda i: (jnp.minimum((i + 1) * per, n // BLOCK - 1), 0)
    cv_in, cv_out, cv_shapes = _convert_specs(convert, layer, n // tm)
    out, *converted = pl.pallas_call(
        functools.partial(_mix_kernel, seq, tm, layer, len(convert)),
        grid=(n // tm,),
        in_specs=[
            pl.BlockSpec(memory_space=pltpu.SMEM),
            pl.BlockSpec((tm, D_MODEL), row),
            pl.BlockSpec((BLOCK, D_MODEL), prev_blk),
            pl.BlockSpec((BLOCK, D_MODEL), next_blk),
            _whole_spec(gain),
            _resident_spec(w_in),
            _resident_spec(wg),
            _whole_spec(ps),
            _resident_spec(wpb),
            _resident_spec(wab),
            _resident_spec(wo),
        ] + cv_in,
        out_specs=[pl.BlockSpec((tm, D_MODEL), row)] + cv_out,
        out_shape=[jax.ShapeDtypeStruct((n, D_MODEL), F32)] + cv_shapes,
        scratch_shapes=[
            pltpu.VMEM((tm + 4 * POOL_HALO, POOL_WIDTH), F32),
            pltpu.VMEM((3 * BLOCK, N_Q_HEADS * BLOCK), F32),
            pltpu.VMEM((len(POOL_WINDOWS) // 2, 2 * POOL_GROUP, 2 * POOL_GROUP), BF16),
            pltpu.VMEM((Q_WIDTH, tm), BF16),
            pltpu.VMEM((tm, D_MODEL), F32),
            pltpu.VMEM((tm, D_MODEL), F32),
            pltpu.VMEM((tm, D_MODEL), BF16),
        ],
        compiler_params=pltpu.CompilerParams(
            dimension_semantics=("arbitrary",), vmem_limit_bytes=VMEM_LIMIT_BYTES),
        name="mix",
    )(sink, h, h, h, gain, w_in, wg, ps, wpb, wab, wo, *convert)
    return out, converted


def _ffn_kernel(final, layer, n_convert, h_ref, gain_ref, wg_ref, wu_ref, wd_ref, fgain_ref, *refs):
    convert_src, (out_ref, *convert_dst), (h_scr, u_scr) = (
        refs[:n_convert], refs[n_convert:2 * n_convert + 1], refs[2 * n_convert + 1:])
    _convert_slabs(convert_src, convert_dst)
    j = pl.program_id(0)

    def normalised():
        return _rms_norm(h_ref[...], gain_ref[layer:layer + 1, :]).astype(BF16)

    @pl.when(j == 0)
    def _():
        h_scr[...] = h_ref[...]
        u_scr[0] = normalised()

    @pl.when(j > 0)
    def _():
        cur, nxt = (j - 1) % 2, j % 2
        h = h_scr[...]
        d_ff = wg_ref.shape[1]
        for c0 in range(0, d_ff, FFN_CHUNK):
            cols = slice(c0, min(c0 + FFN_CHUNK, d_ff))
            u = u_scr[cur]
            gate = _dot(u, wg_ref[:, cols])
            act = ((0.5 * gate) * (1.0 + jnp.tanh(0.5 * gate)) * _dot(u, wu_ref[:, cols])).astype(BF16)
            h = h + _dot(act, wd_ref[cols, :])
            if c0 == 0:
                u_scr[nxt] = normalised()
                h_scr[...] = h_ref[...]
        out_ref[...] = _rms_norm(h, fgain_ref[...]) if final else h


def _ffn_call(h, layer, gain, wg, wu, wd, fgain, final, convert, convert_layer, tm):
    n = h.shape[0]
    n_blocks = n // tm
    staged = lambda j: (jnp.minimum(j, n_blocks - 1), 0)
    finished = lambda j: (jnp.maximum(j - 1, 0), 0)
    cv_in, cv_out, cv_shapes = _convert_specs(convert, convert_layer, n_blocks)
    out, *converted = pl.pallas_call(
        functools.partial(_ffn_kernel, final, layer, len(convert)),
        grid=(n_blocks + 1,),
        in_specs=[
            pl.BlockSpec((tm, D_MODEL), staged),
            _whole_spec(gain),
            _resident_spec(wg),
            _resident_spec(wu),
            _resident_spec(wd),
            pl.BlockSpec((1, D_MODEL), lambda j: (0, 0)),
        ] + cv_in,
        out_specs=[pl.BlockSpec((tm, D_MODEL), finished)] + cv_out,
        out_shape=[jax.ShapeDtypeStruct((n, D_MODEL), F32)] + cv_shapes,
        scratch_shapes=[
            pltpu.VMEM((tm, D_MODEL), F32),
            pltpu.VMEM((2, tm, D_MODEL), BF16),
        ],
        compiler_params=pltpu.CompilerParams(
            dimension_semantics=("arbitrary",), vmem_limit_bytes=VMEM_LIMIT_BYTES),
        name="ffn",
    )(h, gain, wg, wu, wd, fgain, *convert)
    return out, converted


def _convert_kernel(n_convert, *refs):
    _convert_slabs(refs[:n_convert], refs[n_convert:])


def _convert_call(weights, layer):
    cv_in, cv_out, cv_shapes = _convert_specs(weights, layer, CONVERT_STEPS)
    return pl.pallas_call(
        functools.partial(_convert_kernel, len(weights)),
        grid=(CONVERT_STEPS,),
        in_specs=cv_in,
        out_specs=cv_out,
        out_shape=cv_shapes,
        compiler_params=pltpu.CompilerParams(
            dimension_semantics=("arbitrary",), vmem_limit_bytes=VMEM_LIMIT_BYTES),
        name="convert",
    )(*weights)


def kernel(x, norm_mix, w_in, w_pool_group, pool_scale, sink, w_pool_branch, w_attn_branch, w_out,
           norm_ffn, w_ffn_gate, w_ffn_up, w_ffn_down, norm_final):
    batch, seq, _ = x.shape
    depth = w_in.shape[0]
    tm_mix, tm_ffn = 1024, 1024
    assert seq % tm_mix == 0 and tm_mix % BLOCK == 0 and (batch * seq) % tm_ffn == 0

    group_shape = w_pool_group.shape[1:]
    mix_f32 = (w_in, w_pool_group.reshape(depth, -1, group_shape[-1]), w_pool_branch, w_attn_branch, w_out)
    ffn_f32 = (w_ffn_gate, w_ffn_up, w_ffn_down)
    mix_b = _convert_call(mix_f32, 0)

    h = x.reshape(batch * seq, D_MODEL)
    for l in range(depth):
        w_in_b, wg_b, wpb_b, wab_b, wo_b = mix_b
        h, ffn_b = _mix_call(h, l, norm_mix, w_in_b, sink, wg_b.reshape(group_shape), pool_scale, wpb_b, wab_b, wo_b,
                             ffn_f32, seq, tm_mix)
        last = l == depth - 1
        h, mix_b = _ffn_call(h, l, norm_ffn, *ffn_b, norm_final[None, :], last, () if last else mix_f32, l + 1,
                             tm_ffn)
    return h.reshape(batch, seq, D_MODEL)
```

```python
import functools

import jax
import jax.numpy as jnp
from jax import lax
from jax.experimental import pallas as pl
from jax.experimental.pallas import tpu as pltpu

D_MODEL = 1024
POOL_WIDTH = 512
POOL_WINDOWS = (2, 4, 8, 16)
POOL_GROUP = POOL_WIDTH // len(POOL_WINDOWS)
N_Q_HEADS = 8
N_KV_HEADS = 2
HEAD_DIM = 64
Q_GROUP = N_Q_HEADS // N_KV_HEADS
Q_WIDTH = N_Q_HEADS * HEAD_DIM
KV_WIDTH = N_KV_HEADS * HEAD_DIM
WINDOW = 128
BLOCK = 128
GATE_WIDTH = 2 * D_MODEL
IN_WIDTH = POOL_WIDTH + Q_WIDTH + 2 * KV_WIDTH + GATE_WIDTH
EPS = 1e-6
LOG2E = 1.4426950408889634

Q_OFF = POOL_WIDTH
KV_OFF = Q_OFF + Q_WIDTH
GATE_OFF = KV_OFF + 2 * KV_WIDTH

LANES = 128
MXU_COLS = 256
FFN_CHUNK = 4 * MXU_COLS
CONVERT_STEPS = 4
POOL_HALO = 8
BF16_ROWS = 16
assert LANES == 2 * HEAD_DIM and N_KV_HEADS == 2
ALIGNED_HEADS = tuple(h for h in range(N_Q_HEADS) if h % 2 == h // Q_GROUP)
HEAD_ORDER = ALIGNED_HEADS + tuple(h for h in range(N_Q_HEADS) if h not in ALIGNED_HEADS)
VMEM_LIMIT_BYTES = 56 * 1024 * 1024

F32 = jnp.float32
BF16 = jnp.bfloat16


def _rms_norm(x, gain):
    ms = jnp.mean(x * x, axis=-1, keepdims=True)
    return x * lax.rsqrt(ms + EPS) * gain


def _dot(a, b):
    return jnp.dot(a, b, preferred_element_type=F32)


def _dot_tn(a, b):
    return lax.dot_general(a, b, (((0,), (0,)), ((), ())), preferred_element_type=F32)


def _dot_nt(a, b):
    return lax.dot_general(a, b, (((1,), (1,)), ((), ())), preferred_element_type=F32)


def _pool_branch(pos, seq, tm, zext_ref, wg2_ref, scale):
    blocks_per_seq = seq // tm
    h8 = POOL_HALO
    r = lax.broadcasted_iota(jnp.int32, (h8, 1), 0)
    t_first = r
    t_last = seq - h8 + r
    ds = []
    for g, w in enumerate(POOL_WINDOWS):
        cols = slice(g * POOL_GROUP, (g + 1) * POOL_GROUP)
        half = w // 2
        if w == 2:
            s = zext_ref[h8 - 1:h8 - 1 + tm, cols] + zext_ref[h8:h8 + tm, cols]
        else:
            p2 = zext_ref[0:tm + 3 * h8, cols] + zext_ref[1:tm + 3 * h8 + 1, cols]
            if w == 4:
                s = p2[h8 - 2:h8 - 2 + tm] + p2[h8:h8 + tm]
            else:
                p4 = p2[0:tm + 2 * h8] + p2[2:tm + 2 * h8 + 2]
                if w == 8:
                    s = p4[h8 - 4:h8 - 4 + tm] + p4[h8:h8 + tm]
                else:
                    p8 = p4[0:tm + h8] + p4[4:tm + h8 + 4]
                    s = p8[0:tm] + p8[h8:h8 + tm]

        def inv_count(t, at_edge):
            count = (jnp.minimum(t + half, seq) - jnp.maximum(t - half, 0)).astype(F32)
            return jnp.where(at_edge, 1.0 / count, 1.0 / w)

        z = zext_ref[h8:h8 + tm, cols]
        d = jnp.concatenate([
            s[0:h8] * inv_count(t_first, pos == 0) - z[0:h8],
            s[h8:tm - h8] * (1.0 / w) - z[h8:tm - h8],
            s[tm - h8:] * inv_count(t_last, pos == blocks_per_seq - 1) - z[tm - h8:]], axis=0)
        ds.append(d.astype(BF16))
    ys = [_dot(jnp.concatenate(ds[2 * p:2 * p + 2], axis=1), wg2_ref[p]) for p in range(len(ds) // 2)]
    return jnp.concatenate(ys, axis=1) * scale


def _fill_group_pairs(wg_ref, wg2_ref):
    g = POOL_GROUP
    wg2_ref[...] = jnp.zeros(wg2_ref.shape, wg2_ref.dtype)
    for p in range(wg2_ref.shape[0]):
        wg2_ref[p, 0:g, 0:g] = wg_ref[2 * p]
        wg2_ref[p, g:2 * g, g:2 * g] = wg_ref[2 * p + 1]


def _fill_bias(bias_ref):
    c = lax.broadcasted_iota(jnp.int32, (3 * BLOCK, BLOCK), 0)
    a = lax.broadcasted_iota(jnp.int32, (3 * BLOCK, BLOCK), 1)
    absdist = jnp.abs(a - c + BLOCK)
    in_band = absdist <= WINDOW
    absdist_f = absdist.astype(F32)
    for col, h in enumerate(HEAD_ORDER):
        slope = 2.0 ** -(h + 1)
        bias_ref[:, col * BLOCK:(col + 1) * BLOCK] = jnp.where(in_band, (-slope * LOG2E) * absdist_f, -jnp.inf)


def _attention(pos, seq, tm, layer, sink_ref, q, k, k_swapped, v, bias_ref, attn_t_ref, side_work):
    nblk = tm // BLOCK
    lane_half = lax.broadcasted_iota(jnp.int32, (1, LANES), 1) // HEAD_DIM
    sink2 = jnp.concatenate(
        [jnp.full((1, BLOCK), sink_ref[layer, h] * LOG2E, F32) for h in HEAD_ORDER], axis=1)
    edge_first = jnp.where(pos > 0, 0.0, -jnp.inf)
    edge_last = jnp.where(pos < seq // tm - 1, 0.0, -jnp.inf)
    zero = jnp.zeros((), BF16)
    n_aligned = len(ALIGNED_HEADS)
    for n in range(nblk):
        qb = q[n * BLOCK:(n + 1) * BLOCK, :]

        def head_rows(h):
            block = qb[:, (h // 2) * LANES:(h // 2 + 1) * LANES]
            return jnp.where(lane_half == h % 2, block, zero)

        qs = [head_rows(h) for h in HEAD_ORDER]
        rows = slice(n * BLOCK, (n + 3) * BLOCK)
        vb = v[rows]
        s = jnp.concatenate(
            [_dot_nt(k[rows], jnp.concatenate(qs[:n_aligned], axis=0)),
             _dot_nt(k_swapped[rows], jnp.concatenate(qs[n_aligned:], axis=0))], axis=1)
        s = s + bias_ref[...]
        if n == 0:
            s = jnp.concatenate([s[0:BLOCK] + edge_first, s[BLOCK:]], axis=0)
        if n == nblk - 1:
            s = jnp.concatenate([s[:2 * BLOCK], s[2 * BLOCK:] + edge_last], axis=0)
        for work in side_work[n * len(side_work) // nblk:(n + 1) * len(side_work) // nblk]:
            work()
        m = jnp.maximum(jnp.max(s, axis=0, keepdims=True), sink2)
        p = jnp.exp2(s - m)
        denom = jnp.sum(p, axis=0, keepdims=True) + jnp.exp2(sink2 - m)
        p = p.astype(BF16)
        inv = 1.0 / denom
        for kvh in range(N_KV_HEADS):
            cols = [c for c, h in enumerate(HEAD_ORDER) if h // Q_GROUP == kvh]
            p_kv = jnp.concatenate([p[:, c * BLOCK:(c + 1) * BLOCK] for c in cols], axis=1)
            inv_kv = jnp.concatenate([inv[:, c * BLOCK:(c + 1) * BLOCK] for c in cols], axis=1)
            o = _dot_tn(vb[:, kvh * HEAD_DIM:(kvh + 1) * HEAD_DIM], p_kv) * inv_kv
            for j, c in enumerate(cols):
                h = HEAD_ORDER[c]
                attn_t_ref[h * HEAD_DIM:(h + 1) * HEAD_DIM, n * BLOCK:(n + 1) * BLOCK] = (
                    o[:, j * BLOCK:(j + 1) * BLOCK].astype(BF16))


def _convert_slabs(src_refs, dst_refs):
    for src, dst in zip(src_refs, dst_refs):
        dst[...] = src[...].astype(BF16)


def _mix_kernel(seq, tm, layer, n_convert, sink_ref, h_ref, hp_ref, hn_ref, gain_ref, win_ref, wg_ref, ps_ref,
                wpb_ref, wab_ref, wo_ref, *refs):
    convert_src, (out_ref, *convert_dst), scratch = refs[:n_convert], refs[n_convert:2 * n_convert + 1], refs[
        2 * n_convert + 1:]
    zext_ref, bias_ref, wg2_ref, attn_t_ref, gate_ref, pooled_ref, merged_ref = scratch
    _convert_slabs(convert_src, convert_dst)

    @pl.when(pl.program_id(0) == 0)
    def _():
        _fill_bias(bias_ref)
        _fill_group_pairs(wg_ref, wg2_ref)

    blocks_per_seq = seq // tm
    pos = pl.program_id(0) % blocks_per_seq
    gain = gain_ref[layer:layer + 1, :]

    def norm(x):
        return _rms_norm(x, gain).astype(BF16)

    half = tm // 2
    u_lo = jnp.concatenate([norm(hp_ref[...]), norm(h_ref[0:half, :])], axis=0)
    u_hi = jnp.concatenate([norm(h_ref[half:tm, :]), norm(hn_ref[...])], axis=0)
    u = jnp.concatenate([u_lo[BLOCK:], u_hi[:tm - half]], axis=0)

    def project(row_lo, row_hi, col0, width):
        w = win_ref[:, col0:col0 + width]
        return jnp.concatenate([_dot(u_lo[BLOCK + row_lo:], w), _dot(u_hi[:row_hi - half], w)], axis=0)

    zp = project(-BF16_ROWS, tm + BF16_ROWS, 0, POOL_WIDTH)
    h8 = POOL_HALO
    skip = BF16_ROWS - h8
    zext_ref[0:h8, :] = jnp.where(pos > 0, zp[skip:skip + h8], 0.0)
    zext_ref[h8:h8 + tm, :] = zp[BF16_ROWS:BF16_ROWS + tm]
    zext_ref[h8 + tm:2 * h8 + tm, :] = jnp.where(
        pos < blocks_per_seq - 1, zp[BF16_ROWS + tm:BF16_ROWS + tm + h8], 0.0)
    zext_ref[2 * h8 + tm:, :] = jnp.zeros((2 * h8, POOL_WIDTH), F32)

    kv = project(-BLOCK, tm + BLOCK, KV_OFF, 2 * KV_WIDTH)
    k = kv[:, 0:KV_WIDTH].astype(BF16)
    k_swapped = pltpu.roll(kv[:, 0:KV_WIDTH], HEAD_DIM, 1).astype(BF16)
    v = kv[:, KV_WIDTH:2 * KV_WIDTH].astype(BF16)
    q = (project(0, tm, Q_OFF, Q_WIDTH) * (LOG2E * HEAD_DIM ** -0.5)).astype(BF16)

    pooled_feats = []

    def gate(c0):
        return 0.5 + 0.5 * jnp.tanh(0.5 * _dot(u, win_ref[:, GATE_OFF + c0:GATE_OFF + c0 + MXU_COLS]))

    def attn_gate_chunk(c0):
        gate_ref[:, c0:c0 + MXU_COLS] = gate(D_MODEL + c0)

    def pool_chunk(c0):
        if not pooled_feats:
            pooled_feats.append(
                _pool_branch(pos, seq, tm, zext_ref, wg2_ref, ps_ref[layer:layer + 1, :]).astype(BF16))
        cols = slice(c0, c0 + MXU_COLS)
        pooled_ref[:, cols] = gate(c0) * _dot(pooled_feats[0], wpb_ref[:, cols])

    chunks = range(0, D_MODEL, MXU_COLS)
    side_work = ([functools.partial(attn_gate_chunk, c0) for c0 in chunks]
                 + [functools.partial(pool_chunk, c0) for c0 in chunks])
    _attention(pos, seq, tm, layer, sink_ref, q, k, k_swapped, v, bias_ref, attn_t_ref, side_work)
    attn_t = attn_t_ref[...]
    for c0 in chunks:
        cols = slice(c0, c0 + MXU_COLS)
        merged = pooled_ref[:, cols] + gate_ref[:, cols] * _dot_tn(attn_t, wab_ref[:, cols])
        merged_ref[:, cols] = merged.astype(BF16)
    out_ref[...] = h_ref[...] + _dot(merged_ref[...], wo_ref[...])


def _resident_spec(arr):
    zeros = (0,) * arr.ndim
    return pl.BlockSpec(arr.shape, lambda i: zeros, pipeline_mode=pl.Buffered(1))


def _whole_spec(arr):
    return pl.BlockSpec(arr.shape, lambda i: (0, 0))


def _convert_specs(weights, layer, n_steps):
    in_specs, out_specs, out_shapes = [], [], []
    for w in weights:
        _, rows, cols = w.shape
        slab = rows // n_steps
        assert rows % n_steps == 0 and slab % BF16_ROWS == 0
        in_specs.append(pl.BlockSpec((None, slab, cols), lambda i: (layer, i, 0)))
        out_specs.append(pl.BlockSpec((slab, cols), lambda i: (i, 0)))
        out_shapes.append(jax.ShapeDtypeStruct((rows, cols), BF16))
    return in_specs, out_specs, out_shapes


def _mix_call(h, layer, gain, w_in, sink, wg, ps, wpb, wab, wo, convert, seq, tm):
    n = h.shape[0]
    row = lambda i: (i, 0)
    per = tm // BLOCK
    prev_blk = lambda i: (jnp.maximum(i * per - 1, 0), 0)
    next_blk = lambda i: (jnp.minimum((i + 1) * per, n // BLOCK - 1), 0)
    cv_in, cv_out, cv_shapes = _convert_specs(convert, layer, n // tm)
    out, *converted = pl.pallas_call(
        functools.partial(_mix_kernel, seq, tm, layer, len(convert)),
        grid=(n // tm,),
        in_specs=[
            pl.BlockSpec(memory_space=pltpu.SMEM),
            pl.BlockSpec((tm, D_MODEL), row),
            pl.BlockSpec((BLOCK, D_MODEL), prev_blk),
            pl.BlockSpec((BLOCK, D_MODEL), next_blk),
            _whole_spec(gain),
            _resident_spec(w_in),
            _resident_spec(wg),
            _whole_spec(ps),
            _resident_spec(wpb),
            _resident_spec(wab),
            _resident_spec(wo),
        ] + cv_in,
        out_specs=[pl.BlockSpec((tm, D_MODEL), row)] + cv_out,
        out_shape=[jax.ShapeDtypeStruct((n, D_MODEL), F32)] + cv_shapes,
        scratch_shapes=[
            pltpu.VMEM((tm + 4 * POOL_HALO, POOL_WIDTH), F32),
            pltpu.VMEM((3 * BLOCK, N_Q_HEADS * BLOCK), F32),
            pltpu.VMEM((len(POOL_WINDOWS) // 2, 2 * POOL_GROUP, 2 * POOL_GROUP), BF16),
            pltpu.VMEM((Q_WIDTH, tm), BF16),
            pltpu.VMEM((tm, D_MODEL), F32),
            pltpu.VMEM((tm, D_MODEL), F32),
            pltpu.VMEM((tm, D_MODEL), BF16),
        ],
        compiler_params=pltpu.CompilerParams(
            dimension_semantics=("arbitrary",), vmem_limit_bytes=VMEM_LIMIT_BYTES),
        name="mix",
    )(sink, h, h, h, gain, w_in, wg, ps, wpb, wab, wo, *convert)
    return out, converted


def _ffn_kernel(final, layer, n_convert, h_ref, gain_ref, wg_ref, wu_ref, wd_ref, fgain_ref, *refs):
    convert_src, (out_ref, *convert_dst) = refs[:n_convert], refs[n_convert:]
    _convert_slabs(convert_src, convert_dst)
    h = h_ref[...]
    u = _rms_norm(h, gain_ref[layer:layer + 1, :]).astype(BF16)
    d_ff = wg_ref.shape[1]
    for c0 in range(0, d_ff, FFN_CHUNK):
        cols = slice(c0, min(c0 + FFN_CHUNK, d_ff))
        gate = _dot(u, wg_ref[:, cols])
        act = ((0.5 * gate) * (1.0 + jnp.tanh(0.5 * gate)) * _dot(u, wu_ref[:, cols])).astype(BF16)
        h = h + _dot(act, wd_ref[cols, :])
    out_ref[...] = _rms_norm(h, fgain_ref[...]) if final else h


def _ffn_call(h, layer, gain, wg, wu, wd, fgain, final, convert, convert_layer, tm):
    n = h.shape[0]
    row = lambda i: (i, 0)
    cv_in, cv_out, cv_shapes = _convert_specs(convert, convert_layer, n // tm)
    out, *converted = pl.pallas_call(
        functools.partial(_ffn_kernel, final, layer, len(convert)),
        grid=(n // tm,),
        in_specs=[
            pl.BlockSpec((tm, D_MODEL), row),
            _whole_spec(gain),
            _resident_spec(wg),
            _resident_spec(wu),
            _resident_spec(wd),
            pl.BlockSpec((1, D_MODEL), lambda i: (0, 0)),
        ] + cv_in,
        out_specs=[pl.BlockSpec((tm, D_MODEL), row)] + cv_out,
        out_shape=[jax.ShapeDtypeStruct((n, D_MODEL), F32)] + cv_shapes,
        compiler_params=pltpu.CompilerParams(
            dimension_semantics=("arbitrary",), vmem_limit_bytes=VMEM_LIMIT_BYTES),
        name="ffn",
    )(h, gain, wg, wu, wd, fgain, *convert)
    return out, converted


def _convert_kernel(n_convert, *refs):
    _convert_slabs(refs[:n_convert], refs[n_convert:])


def _convert_call(weights, layer):
    cv_in, cv_out, cv_shapes = _convert_specs(weights, layer, CONVERT_STEPS)
    return pl.pallas_call(
        functools.partial(_convert_kernel, len(weights)),
        grid=(CONVERT_STEPS,),
        in_specs=cv_in,
        out_specs=cv_out,
        out_shape=cv_shapes,
        compiler_params=pltpu.CompilerParams(
            dimension_semantics=("arbitrary",), vmem_limit_bytes=VMEM_LIMIT_BYTES),
        name="convert",
    )(*weights)


def kernel(x, norm_mix, w_in, w_pool_group, pool_scale, sink, w_pool_branch, w_attn_branch, w_out,
           norm_ffn, w_ffn_gate, w_ffn_up, w_ffn_down, norm_final):
    batch, seq, _ = x.shape
    depth = w_in.shape[0]
    tm_mix, tm_ffn = 1024, 1024
    assert seq % tm_mix == 0 and tm_mix % BLOCK == 0 and (batch * seq) % tm_ffn == 0

    group_shape = w_pool_group.shape[1:]
    mix_f32 = (w_in, w_pool_group.reshape(depth, -1, group_shape[-1]), w_pool_branch, w_attn_branch, w_out)
    ffn_f32 = (w_ffn_gate, w_ffn_up, w_ffn_down)
    mix_b = tuple(w[0].astype(BF16) for w in mix_f32)
    ffn_b = None

    h = x.reshape(batch * seq, D_MODEL)
    for l in range(depth):
        w_in_b, wg_b, wpb_b, wab_b, wo_b = mix_b
        h, converted = _mix_call(h, l, norm_mix, w_in_b, sink, wg_b.reshape(group_shape), pool_scale, wpb_b, wab_b,
                                 wo_b, ffn_f32 if l == 0 else (), seq, tm_mix)
        ffn_b = converted if l == 0 else ffn_b
        last = l == depth - 1
        h, converted = _ffn_call(h, l, norm_ffn, *ffn_b, norm_final[None, :], last,
                                 () if last else mix_f32 + ffn_f32, l + 1, tm_ffn)
        mix_b, ffn_b = converted[:len(mix_f32)], converted[len(mix_f32):]
    return h.reshape(batch, seq, D_MODEL)
```
